```python
import math
import jax, jax.numpy as jnp
from jax import lax
import numpy as np

D_MODEL = 2048
BATCH = 8
SEQ = 2048
DEPTH = 4

GRID_W = 64
CTX_LEN = 256
N_MIXERS = 3
D_INNER = 2 * D_MODEL
EPS = 1e-6
HY_ORDER = 2
HY_SHORT = 3
HY_EMB = 33
HY_BANDS = (HY_EMB - 1) // 2
HY_FILTER_W = 64
HY_FAST_PCT = 0.3
HY_SLOW_PCT = 1.5
HY_TARGET = 1e-2
CF_KERNEL = 31
RT_HEADS = 8
RT_QK = D_MODEL
RT_V = D_INNER
RT_DK = RT_QK // RT_HEADS
RT_DV = RT_V // RT_HEADS
RT_CHUNK = 128
ROPE_BASE = 10000.0
N_HY = (DEPTH + 2) // 3
N_CF = (DEPTH + 1) // 3
N_RT = DEPTH // 3

kernel_name = "hybrid_hyena_conformer_retention_dit"


def rmsnorm(x, g):
    xf = x.astype(jnp.float32)
    y = xf * lax.rsqrt(jnp.mean(xf * xf, axis=-1, keepdims=True) + EPS)
    return (y * g.astype(jnp.float32)).astype(x.dtype)


def layernorm(x, g, b):
    xf = x.astype(jnp.float32)
    mu = jnp.mean(xf, axis=-1, keepdims=True)
    var = jnp.mean(jnp.square(xf - mu), axis=-1, keepdims=True)
    y = (xf - mu) * lax.rsqrt(var + EPS)
    return (y * g.astype(jnp.float32) + b.astype(jnp.float32)).astype(x.dtype)


def dwconv(x, w, b):
    k = w.shape[0]
    pad = (k - 1) // 2
    y = lax.conv_general_dilated(x, w[:, None, :].astype(x.dtype), window_strides=(1,),
                                 padding=[(pad, pad)], dimension_numbers=('NWC', 'WIO', 'NWC'),
                                 feature_group_count=x.shape[-1])
    return y + b.astype(x.dtype)


def hyena_filters(L, f_w1, f_b1, f_fr1, f_w2, f_b2, f_fr2, f_w3):
    f32 = lambda a: a.astype(jnp.float32)
    t = jnp.linspace(0.0, 1.0, L, dtype=jnp.float32)[:, None]
    w = (2.0 * math.pi / L) * jnp.arange(L, dtype=jnp.float32)[:, None]
    f = jnp.linspace(1e-4, HY_BANDS - 1, HY_BANDS, dtype=jnp.float32)[None, :]
    feats = jnp.concatenate([t, jnp.cos(f * w), -jnp.sin(f * w)], axis=-1)
    h = jnp.sin(f32(f_fr1) * (feats @ f32(f_w1) + f32(f_b1)))
    h = jnp.sin(f32(f_fr2) * (h @ f32(f_w2) + f32(f_b2)))
    h = h @ f32(f_w3)
    deltas = jnp.abs(jnp.linspace(math.log(HY_TARGET) / HY_SLOW_PCT,
                                  math.log(HY_TARGET) / HY_FAST_PCT, D_INNER, dtype=jnp.float32))
    window = jnp.exp(-t * deltas)
    return h.reshape(L, HY_ORDER, 2, D_INNER) * window[:, None, None, :]


def long_conv(z, h_fwd, h_bwd, skip):
    L = z.shape[1]
    g = jnp.concatenate([h_fwd, jnp.zeros_like(h_fwd[:1]), h_bwd[:0:-1]], axis=0)
    spec = jnp.fft.rfft(g, axis=0)
    zf = z.astype(jnp.float32)
    y = jnp.fft.irfft(jnp.fft.rfft(zf, n=2 * L, axis=1) * spec[None], n=2 * L, axis=1)[:, :L]
    return (y + skip.astype(jnp.float32) * zf).astype(z.dtype)


def hyena_seq(h, w_in, conv_w, conv_b, filt, skip, w_out):
    L = h.shape[1]
    proj = h @ w_in
    u, gate = proj[..., :3 * D_INNER], proj[..., 3 * D_INNER:]
    u = dwconv(u, conv_w, conv_b)
    v, x1, x2 = jnp.split(u, 3, axis=-1)
    hf = hyena_filters(L, *filt)
    z = v
    for n, x_n in enumerate((x1, x2)):
        z = x_n * long_conv(z, hf[:, n, 0], hf[:, n, 1], skip[n])
    return (z * jax.nn.silu(gate)) @ w_out


def conformer_seq(h, w_in, dw_w, dw_b, ln_g, ln_b, w_out):
    proj = h @ w_in
    a, b, gate = jnp.split(proj, 3, axis=-1)
    u = a * jax.nn.sigmoid(b)
    u = dwconv(u, dw_w, dw_b)
    u = jax.nn.silu(layernorm(u, ln_g, ln_b))
    return (u * jax.nn.silu(gate)) @ w_out


def rope(x, pos):
    half = x.shape[-1] // 2
    inv = ROPE_BASE ** (-jnp.arange(half, dtype=jnp.float32) / half)
    ang = pos.astype(jnp.float32)[:, None] * inv
    cos, sin = jnp.cos(ang)[:, None, :], jnp.sin(ang)[:, None, :]
    x1, x2 = x[..., :half], x[..., half:]
    return jnp.concatenate([x1 * cos - x2 * sin, x1 * sin + x2 * cos], axis=-1).astype(x.dtype)


def axial_rope(x, rows, cols):
    d2 = x.shape[-1] // 2
    return jnp.concatenate([rope(x[..., :d2], rows), rope(x[..., d2:], cols)], axis=-1)


def to_chunks(t):
    b, L, h, d = t.shape
    return t.reshape(b, L // RT_CHUNK, RT_CHUNK, h, d).transpose(1, 0, 3, 2, 4)


def from_chunks(t):
    n, b, h, c, d = t.shape
    return t.transpose(1, 0, 3, 2, 4).reshape(b, n * c, h, d)


def retention_scan(q, k, v, log_g, s0):
    q, k, v = (a.astype(jnp.float32) for a in (q, k, v))
    idx = jnp.arange(RT_CHUNK, dtype=jnp.float32)
    rel = idx[:, None] - idx[None, :]
    lg = log_g.astype(jnp.float32)
    dmask = jnp.where(rel >= 0, jnp.exp(lg[:, None, None] * jnp.maximum(rel, 0.0)), 0.0)
    q_dec = jnp.exp(lg[:, None] * (idx + 1.0))[:, :, None]
    k_dec = jnp.exp(lg[:, None] * (RT_CHUNK - 1.0 - idx))[:, :, None]
    c_dec = jnp.exp(lg * RT_CHUNK)[:, None, None]

    def step(state, blk):
        qc, kc, vc = blk
        scores = jnp.einsum('bhid,bhjd->bhij', qc, kc) * dmask
        out = (jnp.einsum('bhij,bhjv->bhiv', scores, vc)
               + jnp.einsum('bhid,bhdv->bhiv', qc * q_dec, state))
        state = state * c_dec + jnp.einsum('bhjd,bhjv->bhdv', kc * k_dec, vc)
        return state, out

    s_final, out = lax.scan(step, s0, (to_chunks(q), to_chunks(k), to_chunks(v)))
    return from_chunks(out), s_final


def retention_bidir(q, k, v, log_g_f, log_g_b, s0_f, s0_b):
    o_f, s_f = retention_scan(q, k, v, log_g_f, s0_f)
    o_b, s_b = retention_scan(q[:, ::-1], k[:, ::-1], v[:, ::-1], log_g_b, s0_b)
    return o_f + o_b[:, ::-1], s_f, s_b


def retention_mixer(h_ctx, h_lat, w_in, decay_logit, gn_g, gn_b, w_out):
    log_g = jax.nn.log_sigmoid(decay_logit.astype(jnp.float32))

    def project(h):
        b, L, _ = h.shape
        q, k, v, gate = jnp.split(h @ w_in, [RT_QK, 2 * RT_QK, 2 * RT_QK + RT_V], axis=-1)
        q = q.reshape(b, L, RT_HEADS, RT_DK)
        k = k.reshape(b, L, RT_HEADS, RT_DK) * (RT_DK ** -0.5)
        v = v.reshape(b, L, RT_HEADS, RT_DV)
        return q, k, v, gate

    def finish(o, gate):
        b, L = o.shape[:2]
        mu = jnp.mean(o, axis=-1, keepdims=True)
        var = jnp.mean(jnp.square(o - mu), axis=-1, keepdims=True)
        o = ((o - mu) * lax.rsqrt(var + EPS)).reshape(b, L, RT_V)
        o = o * gn_g.astype(jnp.float32) + gn_b.astype(jnp.float32)
        return (o.astype(gate.dtype) * jax.nn.silu(gate)) @ w_out

    qc, kc, vc, gc = project(h_ctx)
    zero = jnp.zeros((h_ctx.shape[0], RT_HEADS, RT_DK, RT_DV), jnp.float32)
    o_ctx, s_ctx_f, s_ctx_b = retention_bidir(qc, kc, vc, log_g[0], log_g[1], zero, zero)
    ql, kl, vl, gl = project(h_lat)
    L = h_lat.shape[1]
    ROWS = L // GRID_W
    rows = jnp.repeat(jnp.arange(ROWS), GRID_W)
    cols = jnp.tile(jnp.arange(GRID_W), ROWS)
    ql, kl = axial_rope(ql, rows, cols), axial_rope(kl, rows, cols)
    o_lat, _, _ = retention_bidir(ql, kl, vl, log_g[0], log_g[1], s_ctx_f, s_ctx_b)
    return finish(o_ctx, gc), finish(o_lat, gl)


def setup_inputs(seed: int = 0) -> dict:
    key = jax.random.key(seed)
    ks = iter(jax.random.split(key, 40))
    nrm = lambda shape, s: s * jax.random.normal(next(ks), shape, jnp.float32)
    D, E = D_MODEL, D_INNER
    decay_logit0 = np.log(2.0 ** (5.0 + np.arange(RT_HEADS)) - 1.0).astype(np.float32)
    return {
        'x': nrm((BATCH, SEQ, D), 1.0),
        'c': nrm((BATCH, D), 1.0),
        'ctx': nrm((BATCH, CTX_LEN, D), 1.0),
        'c_ctx': nrm((D,), 1.0),
        'ada_w': nrm((DEPTH, D, 3 * D), 0.5 * D ** -0.5),
        'ada_b': nrm((DEPTH, 3 * D), 0.01),
        'norm_g': 1.0 + nrm((DEPTH, D), 0.02),
        'final_norm_g': 1.0 + nrm((D,), 0.02),
        'hy_w_in': nrm((N_HY, D, 4 * E), D ** -0.5),
        'hy_conv_w': nrm((N_HY, HY_SHORT, 3 * E), HY_SHORT ** -0.5),
        'hy_conv_b': nrm((N_HY, 3 * E), 0.01),
        'hy_f_w1': nrm((N_HY, HY_EMB, HY_FILTER_W), HY_EMB ** -0.5),
        'hy_f_b1': nrm((N_HY, HY_FILTER_W), 0.1),
        'hy_f_fr1': 1.0 + nrm((N_HY, HY_FILTER_W), 0.1),
        'hy_f_w2': nrm((N_HY, HY_FILTER_W, HY_FILTER_W), HY_FILTER_W ** -0.5),
        'hy_f_b2': nrm((N_HY, HY_FILTER_W), 0.1),
        'hy_f_fr2': 1.0 + nrm((N_HY, HY_FILTER_W), 0.1),
        'hy_f_w3': nrm((N_HY, HY_FILTER_W, HY_ORDER * 2 * E), 0.03 * HY_FILTER_W ** -0.5),
        'hy_skip': nrm((N_HY, HY_ORDER, E), 1.0),
        'hy_w_out': nrm((N_HY, E, D), E ** -0.5),
        'cf_w_in': nrm((N_CF, D, 3 * E), D ** -0.5),
        'cf_dw_w': nrm((N_CF, CF_KERNEL, E), CF_KERNEL ** -0.5),
        'cf_dw_b': nrm((N_CF, E), 0.01),
        'cf_ln_g': 1.0 + nrm((N_CF, E), 0.02),
        'cf_ln_b': nrm((N_CF, E), 0.01),
        'cf_w_out': nrm((N_CF, E, D), E ** -0.5),
        'rt_w_in': nrm((N_RT, D, 2 * RT_QK + 2 * RT_V), D ** -0.5),
        'rt_decay_logit': jnp.asarray(decay_logit0) + nrm((N_RT, 2, RT_HEADS), 0.1),
        'rt_gn_g': 1.0 + nrm((N_RT, RT_V), 0.02),
        'rt_gn_b': nrm((N_RT, RT_V), 0.01),
        'rt_w_out': nrm((N_RT, RT_V, D), RT_V ** -0.5),
    }


def reference(x, c, ctx, c_ctx, ada_w, ada_b, norm_g, final_norm_g,
              hy_w_in, hy_conv_w, hy_conv_b, hy_f_w1, hy_f_b1, hy_f_fr1, hy_f_w2, hy_f_b2,
              hy_f_fr2, hy_f_w3, hy_skip, hy_w_out,
              cf_w_in, cf_dw_w, cf_dw_b, cf_ln_g, cf_ln_b, cf_w_out,
              rt_w_in, rt_decay_logit, rt_gn_g, rt_gn_b, rt_w_out):
    mod_lat = jax.nn.silu(c)[:, None, :]
    mod_ctx = jax.nn.silu(c_ctx)[None, None, :]
    for i in range(DEPTH):
        kind, j = i % N_MIXERS, i // N_MIXERS
        last = i == DEPTH - 1
        need_ctx = (not last) or kind == 2
        shift_l, scale_l, gate_l = jnp.split(mod_lat @ ada_w[i] + ada_b[i], 3, axis=-1)
        h_lat = rmsnorm(x, norm_g[i]) * (1.0 + scale_l) + shift_l
        y_ctx = None
        if need_ctx:
            shift_c, scale_c, gate_c = jnp.split(mod_ctx @ ada_w[i] + ada_b[i], 3, axis=-1)
            h_ctx = rmsnorm(ctx, norm_g[i]) * (1.0 + scale_c) + shift_c
        if kind == 0:
            filt = (hy_f_w1[j], hy_f_b1[j], hy_f_fr1[j], hy_f_w2[j], hy_f_b2[j], hy_f_fr2[j], hy_f_w3[j])
            p = (hy_w_in[j], hy_conv_w[j], hy_conv_b[j], filt, hy_skip[j], hy_w_out[j])
            y_lat = hyena_seq(h_lat, *p)
            if need_ctx:
                y_ctx = hyena_seq(h_ctx, *p)
        elif kind == 1:
            p = (cf_w_in[j], cf_dw_w[j], cf_dw_b[j], cf_ln_g[j], cf_ln_b[j], cf_w_out[j])
            y_lat = conformer_seq(h_lat, *p)
            if need_ctx:
                y_ctx = conformer_seq(h_ctx, *p)
        else:
            y_ctx, y_lat = retention_mixer(h_ctx, h_lat, rt_w_in[j], rt_decay_logit[j],
                                           rt_gn_g[j], rt_gn_b[j], rt_w_out[j])
        x = x + gate_l * y_lat
        if not last:
            ctx = ctx + gate_c * y_ctx
    return rmsnorm(x, final_norm_g)
```

```python
import math
import jax, jax.numpy as jnp
from jax import lax
import numpy as np
from jax.experimental import pallas as pl
from jax.experimental.pallas import tpu as pltpu

D_MODEL = 2048
BATCH = 8
SEQ = 2048
DEPTH = 4

GRID_W = 64
CTX_LEN = 256
N_MIXERS = 3
D_INNER = 2 * D_MODEL
EPS = 1e-6
HY_ORDER = 2
HY_SHORT = 3
HY_EMB = 33
HY_BANDS = (HY_EMB - 1) // 2
HY_FILTER_W = 64
HY_FAST_PCT = 0.3
HY_SLOW_PCT = 1.5
HY_TARGET = 1e-2
CF_KERNEL = 31
RT_HEADS = 8
RT_QK = D_MODEL
RT_V = D_INNER
RT_DK = RT_QK // RT_HEADS
RT_DV = RT_V // RT_HEADS
RT_CHUNK = 128
ROPE_BASE = 10000.0


def rmsnorm(x, g):
    xf = x.astype(jnp.float32)
    y = xf * lax.rsqrt(jnp.mean(xf * xf, axis=-1, keepdims=True) + EPS)
    return (y * g.astype(jnp.float32)).astype(x.dtype)


def layernorm(x, g, b):
    xf = x.astype(jnp.float32)
    mu = jnp.mean(xf, axis=-1, keepdims=True)
    var = jnp.mean(jnp.square(xf - mu), axis=-1, keepdims=True)
    y = (xf - mu) * lax.rsqrt(var + EPS)
    return (y * g.astype(jnp.float32) + b.astype(jnp.float32)).astype(x.dtype)


def dwconv(x, w, b):
    k = w.shape[0]
    pad = (k - 1) // 2
    y = lax.conv_general_dilated(x, w[:, None, :].astype(x.dtype), window_strides=(1,),
                                 padding=[(pad, pad)], dimension_numbers=('NWC', 'WIO', 'NWC'),
                                 feature_group_count=x.shape[-1])
    return y + b.astype(x.dtype)


def hyena_filters(L, f_w1, f_b1, f_fr1, f_w2, f_b2, f_fr2, f_w3):
    f32 = lambda a: a.astype(jnp.float32)
    t = jnp.linspace(0.0, 1.0, L, dtype=jnp.float32)[:, None]
    w = (2.0 * math.pi / L) * jnp.arange(L, dtype=jnp.float32)[:, None]
    f = jnp.linspace(1e-4, HY_BANDS - 1, HY_BANDS, dtype=jnp.float32)[None, :]
    feats = jnp.concatenate([t, jnp.cos(f * w), -jnp.sin(f * w)], axis=-1)
    h = jnp.sin(f32(f_fr1) * (feats @ f32(f_w1) + f32(f_b1)))
    h = jnp.sin(f32(f_fr2) * (h @ f32(f_w2) + f32(f_b2)))
    h = h @ f32(f_w3)
    deltas = jnp.abs(jnp.linspace(math.log(HY_TARGET) / HY_SLOW_PCT,
                                  math.log(HY_TARGET) / HY_FAST_PCT, D_INNER, dtype=jnp.float32))
    window = jnp.exp(-t * deltas)
    return h.reshape(L, HY_ORDER, 2, D_INNER) * window[:, None, None, :]


def long_conv(z, h_fwd, h_bwd, skip):
    L = z.shape[1]
    g = jnp.concatenate([h_fwd, jnp.zeros_like(h_fwd[:1]), h_bwd[:0:-1]], axis=0)
    spec = jnp.fft.rfft(g, axis=0)
    zf = z.astype(jnp.float32)
    y = jnp.fft.irfft(jnp.fft.rfft(zf, n=2 * L, axis=1) * spec[None], n=2 * L, axis=1)[:, :L]
    return (y + skip.astype(jnp.float32) * zf).astype(z.dtype)


def hyena_seq(h, w_in, conv_w, conv_b, filt, skip, w_out):
    L = h.shape[1]
    proj = h @ w_in
    u, gate = proj[..., :3 * D_INNER], proj[..., 3 * D_INNER:]
    u = dwconv(u, conv_w, conv_b)
    v, x1, x2 = jnp.split(u, 3, axis=-1)
    hf = hyena_filters(L, *filt)
    z = v
    for n, x_n in enumerate((x1, x2)):
        z = x_n * long_conv(z, hf[:, n, 0], hf[:, n, 1], skip[n])
    return (z * jax.nn.silu(gate)) @ w_out


def conformer_seq(h, w_in, dw_w, dw_b, ln_g, ln_b, w_out):
    proj = h @ w_in
    a, b, gate = jnp.split(proj, 3, axis=-1)
    u = a * jax.nn.sigmoid(b)
    u = dwconv(u, dw_w, dw_b)
    u = jax.nn.silu(layernorm(u, ln_g, ln_b))
    return (u * jax.nn.silu(gate)) @ w_out


def rope(x, pos):
    half = x.shape[-1] // 2
    inv = ROPE_BASE ** (-jnp.arange(half, dtype=jnp.float32) / half)
    ang = pos.astype(jnp.float32)[:, None] * inv
    cos, sin = jnp.cos(ang)[:, None, :], jnp.sin(ang)[:, None, :]
    x1, x2 = x[..., :half], x[..., half:]
    return jnp.concatenate([x1 * cos - x2 * sin, x1 * sin + x2 * cos], axis=-1).astype(x.dtype)


def axial_rope(x, rows, cols):
    d2 = x.shape[-1] // 2
    return jnp.concatenate([rope(x[..., :d2], rows), rope(x[..., d2:], cols)], axis=-1)


def to_chunks(t):
    b, L, h, d = t.shape
    return t.reshape(b, L // RT_CHUNK, RT_CHUNK, h, d).transpose(1, 0, 3, 2, 4)


def from_chunks(t):
    n, b, h, c, d = t.shape
    return t.transpose(1, 0, 3, 2, 4).reshape(b, n * c, h, d)


def retention_scan(q, k, v, log_g, s0):
    q, k, v = (a.astype(jnp.float32) for a in (q, k, v))
    idx = jnp.arange(RT_CHUNK, dtype=jnp.float32)
    rel = idx[:, None] - idx[None, :]
    lg = log_g.astype(jnp.float32)
    dmask = jnp.where(rel >= 0, jnp.exp(lg[:, None, None] * jnp.maximum(rel, 0.0)), 0.0)
    q_dec = jnp.exp(lg[:, None] * (idx + 1.0))[:, :, None]
    k_dec = jnp.exp(lg[:, None] * (RT_CHUNK - 1.0 - idx))[:, :, None]
    c_dec = jnp.exp(lg * RT_CHUNK)[:, None, None]

    def step(state, blk):
        qc, kc, vc = blk
        scores = jnp.einsum('bhid,bhjd->bhij', qc, kc) * dmask
        out = (jnp.einsum('bhij,bhjv->bhiv', scores, vc)
               + jnp.einsum('bhid,bhdv->bhiv', qc * q_dec, state))
        state = state * c_dec + jnp.einsum('bhjd,bhjv->bhdv', kc * k_dec, vc)
        return state, out

    s_final, out = lax.scan(step, s0, (to_chunks(q), to_chunks(k), to_chunks(v)))
    return from_chunks(out), s_final


def retention_bidir(q, k, v, log_g_f, log_g_b, s0_f, s0_b):
    o_f, s_f = retention_scan(q, k, v, log_g_f, s0_f)
    o_b, s_b = retention_scan(q[:, ::-1], k[:, ::-1], v[:, ::-1], log_g_b, s0_b)
    return o_f + o_b[:, ::-1], s_f, s_b


def retention_mixer(h_ctx, h_lat, w_in, decay_logit, gn_g, gn_b, w_out):
    log_g = jax.nn.log_sigmoid(decay_logit.astype(jnp.float32))

    def project(h):
        b, L, _ = h.shape
        q, k, v, gate = jnp.split(h @ w_in, [RT_QK, 2 * RT_QK, 2 * RT_QK + RT_V], axis=-1)
        q = q.reshape(b, L, RT_HEADS, RT_DK)
        k = k.reshape(b, L, RT_HEADS, RT_DK) * (RT_DK ** -0.5)
        v = v.reshape(b, L, RT_HEADS, RT_DV)
        return q, k, v, gate

    def finish(o, gate):
        b, L = o.shape[:2]
        mu = jnp.mean(o, axis=-1, keepdims=True)
        var = jnp.mean(jnp.square(o - mu), axis=-1, keepdims=True)
        o = ((o - mu) * lax.rsqrt(var + EPS)).reshape(b, L, RT_V)
        o = o * gn_g.astype(jnp.float32) + gn_b.astype(jnp.float32)
        return (o.astype(gate.dtype) * jax.nn.silu(gate)) @ w_out

    qc, kc, vc, gc = project(h_ctx)
    zero = jnp.zeros((h_ctx.shape[0], RT_HEADS, RT_DK, RT_DV), jnp.float32)
    o_ctx, s_ctx_f, s_ctx_b = retention_bidir(qc, kc, vc, log_g[0], log_g[1], zero, zero)
    ql, kl, vl, gl = project(h_lat)
    L = h_lat.shape[1]
    ROWS = L // GRID_W
    rows = jnp.repeat(jnp.arange(ROWS), GRID_W)
    cols = jnp.tile(jnp.arange(GRID_W), ROWS)
    ql, kl = axial_rope(ql, rows, cols), axial_rope(kl, rows, cols)
    o_lat, _, _ = retention_bidir(ql, kl, vl, log_g[0], log_g[1], s_ctx_f, s_ctx_b)
    return finish(o_ctx, gc), finish(o_lat, gl)


def _rms_kernel(x_ref, g_ref, o_ref):
    xf = x_ref[...]
    y = xf * lax.rsqrt(jnp.mean(xf * xf, axis=-1, keepdims=True) + EPS)
    o_ref[...] = y * g_ref[...]


def _final_rmsnorm(x, g):
    b, L, d = x.shape
    x2 = x.reshape(b * L, d)
    tm = 512
    out = pl.pallas_call(
        _rms_kernel,
        grid=(b * L // tm,),
        in_specs=[pl.BlockSpec((tm, d), lambda i: (i, 0)), pl.BlockSpec((1, d), lambda i: (0, 0))],
        out_specs=pl.BlockSpec((tm, d), lambda i: (i, 0)),
        out_shape=jax.ShapeDtypeStruct((b * L, d), jnp.float32),
    )(x2, g.reshape(1, d))
    return out.reshape(b, L, d)


def kernel(x, c, ctx, c_ctx, ada_w, ada_b, norm_g, final_norm_g,
           hy_w_in, hy_conv_w, hy_conv_b, hy_f_w1, hy_f_b1, hy_f_fr1, hy_f_w2, hy_f_b2,
           hy_f_fr2, hy_f_w3, hy_skip, hy_w_out,
           cf_w_in, cf_dw_w, cf_dw_b, cf_ln_g, cf_ln_b, cf_w_out,
           rt_w_in, rt_decay_logit, rt_gn_g, rt_gn_b, rt_w_out):
    mod_lat = jax.nn.silu(c)[:, None, :]
    mod_ctx = jax.nn.silu(c_ctx)[None, None, :]
    for i in range(DEPTH):
        kind, j = i % N_MIXERS, i // N_MIXERS
        last = i == DEPTH - 1
        need_ctx = (not last) or kind == 2
        shift_l, scale_l, gate_l = jnp.split(mod_lat @ ada_w[i] + ada_b[i], 3, axis=-1)
        h_lat = rmsnorm(x, norm_g[i]) * (1.0 + scale_l) + shift_l
        y_ctx = None
        if need_ctx:
            shift_c, scale_c, gate_c = jnp.split(mod_ctx @ ada_w[i] + ada_b[i], 3, axis=-1)
            h_ctx = rmsnorm(ctx, norm_g[i]) * (1.0 + scale_c) + shift_c
        if kind == 0:
            filt = (hy_f_w1[j], hy_f_b1[j], hy_f_fr1[j], hy_f_w2[j], hy_f_b2[j], hy_f_fr2[j], hy_f_w3[j])
            p = (hy_w_in[j], hy_conv_w[j], hy_conv_b[j], filt, hy_skip[j], hy_w_out[j])
            y_lat = hyena_seq(h_lat, *p)
            if need_ctx:
                y_ctx = hyena_seq(h_ctx, *p)
        elif kind == 1:
            p = (cf_w_in[j], cf_dw_w[j], cf_dw_b[j], cf_ln_g[j], cf_ln_b[j], cf_w_out[j])
            y_lat = conformer_seq(h_lat, *p)
            if need_ctx:
                y_ctx = conformer_seq(h_ctx, *p)
        else:
            y_ctx, y_lat = retention_mixer(h_ctx, h_lat, rt_w_in[j], rt_decay_logit[j],
                                           rt_gn_g[j], rt_gn_b[j], rt_w_out[j])
        x = x + gate_l * y_lat
        if not last:
            ctx = ctx + gate_c * y_ctx
    return _final_rmsnorm(x, final_norm_g)
```

```python
import functools
import math

import jax
import jax.numpy as jnp
from jax import lax
from jax.experimental import pallas as pl
from jax.experimental.pallas import tpu as pltpu

F32 = jnp.float32
BF16 = jnp.bfloat16

EPS = 1e-6
N_MIXERS = 3
GRID_W = 64
HY_ORDER = 2
HY_EMB = 33
HY_BANDS = (HY_EMB - 1) // 2
HY_FAST_PCT = 0.3
HY_SLOW_PCT = 1.5
HY_TARGET = 1e-2
CF_KERNEL = 31
CF_HALO = 16
RT_HEADS = 8
RT_CHUNK = 128
ROPE_BASE = 10000.0

V7X_VMEM_BYTES = 64 * 1024 * 1024
VMEM_LIMIT_BYTES = V7X_VMEM_BYTES - 8 * 1024 * 1024
LANE = 128


def _params(*semantics):
    return pltpu.CompilerParams(dimension_semantics=semantics, vmem_limit_bytes=VMEM_LIMIT_BYTES)


def _resident(block_shape, index_map):
    return pl.BlockSpec(block_shape, index_map, pipeline_mode=pl.Buffered(1))


def _mod_kernel(c_ref, w_ref, b_ref, o_ref):
    c = c_ref[...]
    m = (c * jax.nn.sigmoid(c)).astype(BF16)
    o_ref[...] = jnp.dot(m, w_ref[...].astype(BF16), preferred_element_type=F32) + b_ref[...]


def _modulation(c, c_ctx, ada_w, ada_b):
    depth, d, n3 = ada_w.shape
    b = c.shape[0]
    rows = -(-(b + 1) // 8) * 8
    cc = jnp.zeros((rows, d), F32).at[:b].set(c).at[b].set(c_ctx)
    tn = 1536
    out = pl.pallas_call(
        _mod_kernel,
        grid=(depth, n3 // tn),
        in_specs=[pl.BlockSpec((rows, d), lambda l, n: (0, 0)),
                  pl.BlockSpec((None, d, tn), lambda l, n: (l, 0, n)),
                  pl.BlockSpec((None, 1, tn), lambda l, n: (l, 0, n))],
        out_specs=pl.BlockSpec((None, rows, tn), lambda l, n: (l, 0, n)),
        out_shape=jax.ShapeDtypeStruct((depth, rows, n3), F32),
        compiler_params=_params("arbitrary", "arbitrary"),
        name="adaln_modulation",
    )(cc, ada_w, ada_b.reshape(depth, 1, n3))
    return out.reshape(depth, rows, 1, n3)


def _normmod_kernel(x_ref, g_ref, sh_ref, sc_ref, o_ref):
    x = x_ref[...]
    y = x * lax.rsqrt(jnp.mean(x * x, axis=-1, keepdims=True) + EPS)
    o_ref[...] = ((y * g_ref[...]) * (1.0 + sc_ref[...]) + sh_ref[...]).astype(o_ref.dtype)


def _normmod(x, norm_g, mod, layer, row_of):
    b, L, d = x.shape
    tl = min(L, 512)
    return pl.pallas_call(
        _normmod_kernel,
        grid=(b, L // tl),
        in_specs=[pl.BlockSpec((None, tl, d), lambda bi, i: (bi, i, 0)),
                  pl.BlockSpec((1, d), lambda bi, i: (0, 0)),
                  pl.BlockSpec((None, None, 1, d), lambda bi, i: (layer, row_of(bi), 0, 0)),
                  pl.BlockSpec((None, None, 1, d), lambda bi, i: (layer, row_of(bi), 0, 1))],
        out_specs=pl.BlockSpec((None, tl, d), lambda bi, i: (bi, i, 0)),
        out_shape=jax.ShapeDtypeStruct((b, L, d), BF16),
        compiler_params=_params("arbitrary", "arbitrary"),
        name="normmod",
    )(x, norm_g.reshape(1, d), mod, mod)


def _silu(x):
    return x * jax.nn.sigmoid(x)


def _proj_kernel(*refs, mode, seq_len, scale):
    h_ref, w_ref = refs[0], refs[1]
    o_ref = refs[-1]
    acc = jnp.dot(h_ref[...], w_ref[...], preferred_element_type=F32)
    tm, tn = acc.shape
    if mode == "plain":
        y = acc
    elif mode == "scale":
        y = acc * scale
    elif mode == "silu":
        y = _silu(acc)
    elif mode == "glu":
        gate = jnp.dot(h_ref[...], refs[2][...], preferred_element_type=F32)
        y = acc * jax.nn.sigmoid(gate)
    elif mode == "conv3":
        cw = refs[2][...]
        row = lax.rem(lax.broadcasted_iota(jnp.int32, acc.shape, 0), seq_len)
        prev = jnp.where(row == 0, 0.0, pltpu.roll(acc, 1, axis=0))
        nxt = jnp.where(row == seq_len - 1, 0.0, pltpu.roll(acc, tm - 1, axis=0))
        y = prev * cw[0:1] + acc * cw[1:2] + nxt * cw[2:3] + refs[3][...]
    elif mode == "rope":
        cos, sin = refs[2][...], refs[3][...]
        dk = cos.shape[-1]
        quarter = dk // 4
        lane = lax.broadcasted_iota(jnp.int32, (tm, dk), 1)
        first = lax.rem(lane, 2 * quarter) < quarter
        parts = []
        for gi in range(tn // dk):
            xg = acc[:, gi * dk:(gi + 1) * dk]
            swapped = jnp.where(first, pltpu.roll(xg, dk - quarter, axis=1), pltpu.roll(xg, quarter, axis=1))
            parts.append((xg * cos + swapped * sin) * scale)
        y = parts[0] if len(parts) == 1 else jnp.concatenate(parts, axis=1)
    else:
        raise ValueError(mode)
    o_ref[...] = y.astype(o_ref.dtype)


def _proj(h2, w, col0, ncols, mode, *, seq_len, extra=(), scale=1.0, tn=512):
    m_rows, d = h2.shape
    n_seq = m_rows // seq_len
    k = 1
    if mode != "rope":
        for cand in range(1, n_seq + 1):
            if n_seq % cand == 0 and seq_len * cand <= 2048:
                k = cand
    tm = seq_len * k
    c0 = col0 // tn
    in_specs = [pl.BlockSpec((tm, d), lambda m, n: (m, 0)),
                pl.BlockSpec((d, tn), lambda m, n: (0, c0 + n))]
    args = [h2, w]
    if mode == "glu":
        gate_c0 = extra[0] // tn
        in_specs.append(pl.BlockSpec((d, tn), lambda m, n: (0, gate_c0 + n)))
        args.append(w)
    elif mode == "conv3":
        cw, cb = extra
        in_specs += [pl.BlockSpec((cw.shape[0], tn), lambda m, n: (0, n)),
                     pl.BlockSpec((1, tn), lambda m, n: (0, n))]
        args += [cw, cb.reshape(1, -1)]
    elif mode == "rope":
        cos, sin = extra
        dk = cos.shape[-1]
        in_specs += [pl.BlockSpec((tm, dk), lambda m, n: (0, 0)),
                     pl.BlockSpec((tm, dk), lambda m, n: (0, 0))]
        args += [cos, sin]
    return pl.pallas_call(
        functools.partial(_proj_kernel, mode=mode, seq_len=seq_len, scale=scale),
        grid=(m_rows // tm, ncols // tn),
        in_specs=in_specs,
        out_specs=pl.BlockSpec((tm, tn), lambda m, n: (m, n)),
        out_shape=jax.ShapeDtypeStruct((m_rows, ncols), BF16),
        compiler_params=_params("arbitrary", "arbitrary"),
        name="proj_" + mode,
    )(*args)


def _outproj_kernel(u_ref, w_ref, x_ref, gate_ref, o_ref):
    acc = jnp.dot(u_ref[...], w_ref[...], preferred_element_type=F32)
    o_ref[...] = x_ref[...] + gate_ref[...] * acc


def _outproj(u, w, x, mod, layer, row_of):
    b, L, e = u.shape
    d = x.shape[-1]
    tm = min(L, 1024)
    tn = 512
    gate_c0 = 2 * d // tn
    return pl.pallas_call(
        _outproj_kernel,
        grid=(b, L // tm, d // tn),
        in_specs=[pl.BlockSpec((None, tm, e), lambda bi, m, n: (bi, m, 0)),
                  pl.BlockSpec((e, tn), lambda bi, m, n: (0, n)),
                  pl.BlockSpec((None, tm, tn), lambda bi, m, n: (bi, m, n)),
                  pl.BlockSpec((None, None, 1, tn), lambda bi, m, n: (layer, row_of(bi), 0, gate_c0 + n))],
        out_specs=pl.BlockSpec((None, tm, tn), lambda bi, m, n: (bi, m, n)),
        out_shape=jax.ShapeDtypeStruct(x.shape, F32),
        compiler_params=_params("arbitrary", "arbitrary", "arbitrary"),
        name="outproj",
    )(u, w, x, mod)


def _filter_kernel(feat_ref, w1_ref, b1_ref, fr1_ref, w2_ref, b2_ref, fr2_ref,
                   wf0_ref, wb0_ref, wf1_ref, wb1_ref, dl_ref, s_ref, d_ref):
    hp = lax.Precision.HIGHEST
    z1 = jnp.dot(feat_ref[...], w1_ref[...], precision=hp, preferred_element_type=F32) + b1_ref[...]
    h1 = jnp.sin(fr1_ref[...] * z1)
    z2 = jnp.dot(h1, w2_ref[...], precision=hp, preferred_element_type=F32) + b2_ref[...]
    h2 = jnp.sin(fr2_ref[...] * z2)
    L = feat_ref.shape[0]
    te = dl_ref.shape[-1]
    row = lax.broadcasted_iota(jnp.int32, (L, te), 0)
    t = row.astype(F32) * (1.0 / (L - 1))
    window = jnp.exp(-t * dl_ref[...])
    for n, (wf_ref, wb_ref) in enumerate(((wf0_ref, wb0_ref), (wf1_ref, wb1_ref))):
        hf = jnp.dot(h2, wf_ref[...], precision=hp, preferred_element_type=F32) * window
        hb = jnp.dot(h2, wb_ref[...], precision=hp, preferred_element_type=F32) * window
        hb = jnp.where(row == 0, 0.0, hb)
        s_ref[n] = (hf + hb).astype(s_ref.dtype)
        d_ref[n] = (hf - hb).astype(d_ref.dtype)


def _spectrum_kernel(cs_ref, sv_ref, dv_ref, hr_ref, hn_ref, hq_ref):
    L = cs_ref.shape[0]
    inv_n = 1.0 / (2 * L)
    sv = sv_ref[...]
    row = lax.broadcasted_iota(jnp.int32, sv.shape, 0)
    hr = jnp.dot(cs_ref[:, 0:L], sv, preferred_element_type=F32)
    hr_ref[...] = hr * jnp.where(row == 0, inv_n, 2.0 * inv_n)
    hn_ref[...] = jnp.dot(cs_ref[:, L:2 * L], dv_ref[...], preferred_element_type=F32) * (2.0 * inv_n)
    sign = (1 - 2 * (row & 1)).astype(F32)
    nyq = jnp.sum(sign * sv.astype(F32), axis=0, keepdims=True) * inv_n
    hq_ref[...] = jnp.broadcast_to(nyq, hq_ref.shape)


def _dft_bases(L):
    k = jnp.arange(L, dtype=jnp.int32)
    m = (k[:, None] * k[None, :]) % (2 * L)
    ang = m.astype(F32) * (math.pi / L)
    return jnp.concatenate([jnp.cos(ang), jnp.sin(ang)], axis=1).astype(BF16)


def _hyena_spectrum(L, e, cs, f_w1, f_b1, f_fr1, f_w2, f_b2, f_fr2, f_w3):
    fw = f_w1.shape[1]
    t = jnp.linspace(0.0, 1.0, L, dtype=F32)[:, None]
    w = (2.0 * math.pi / L) * jnp.arange(L, dtype=F32)[:, None]
    f = jnp.linspace(1e-4, HY_BANDS - 1, HY_BANDS, dtype=F32)[None, :]
    feats = jnp.concatenate([t, jnp.cos(f * w), -jnp.sin(f * w)], axis=-1)
    feats = jnp.pad(feats, ((0, 0), (0, LANE - HY_EMB)))
    w1 = jnp.pad(f_w1, ((0, LANE - HY_EMB), (0, 0)))
    deltas = jnp.abs(jnp.linspace(math.log(HY_TARGET) / HY_SLOW_PCT,
                                  math.log(HY_TARGET) / HY_FAST_PCT, e, dtype=F32)).reshape(1, e)
    te = 512
    ne = e // te
    small = lambda shape: pl.BlockSpec(shape, lambda i: (0, 0))
    w3spec = lambda q: pl.BlockSpec((fw, te), lambda i: (0, q * ne + i))
    s_t, d_t = pl.pallas_call(
        _filter_kernel,
        grid=(ne,),
        in_specs=[small((L, LANE)), small((LANE, fw)), small((1, fw)), small((1, fw)),
                  small((fw, fw)), small((1, fw)), small((1, fw)),
                  w3spec(0), w3spec(1), w3spec(2), w3spec(3),
                  pl.BlockSpec((1, te), lambda i: (0, i))],
        out_specs=[pl.BlockSpec((HY_ORDER, L, te), lambda i: (0, 0, i)),
                   pl.BlockSpec((HY_ORDER, L, te), lambda i: (0, 0, i))],
        out_shape=[jax.ShapeDtypeStruct((HY_ORDER, L, e), BF16),
                   jax.ShapeDtypeStruct((HY_ORDER, L, e), BF16)],
        compiler_params=_params("arbitrary"),
        name="hyena_filter",
    )(feats, w1, f_b1.reshape(1, fw), f_fr1.reshape(1, fw), f_w2, f_b2.reshape(1, fw),
      f_fr2.reshape(1, fw), f_w3, f_w3, f_w3, f_w3, deltas)

    ts = 256
    hr, hn, hq = pl.pallas_call(
        _spectrum_kernel,
        grid=(HY_ORDER, e // ts),
        in_specs=[_resident((L, 2 * L), lambda n, i: (0, 0)),
                  pl.BlockSpec((None, L, ts), lambda n, i: (n, 0, i)),
                  pl.BlockSpec((None, L, ts), lambda n, i: (n, 0, i))],
        out_specs=[pl.BlockSpec((None, L, ts), lambda n, i: (n, 0, i)),
                   pl.BlockSpec((None, L, ts), lambda n, i: (n, 0, i)),
                   pl.BlockSpec((None, 8, ts), lambda n, i: (n, 0, i))],
        out_shape=[jax.ShapeDtypeStruct((HY_ORDER, L, e), F32),
                   jax.ShapeDtypeStruct((HY_ORDER, L, e), F32),
                   jax.ShapeDtypeStruct((HY_ORDER, 8, e), F32)],
        compiler_params=_params("arbitrary", "arbitrary"),
        name="hyena_spectrum",
    )(cs, s_t, d_t)
    return hr, hn, hq


def _longconv_kernel(cs_ref, v_ref, x1_ref, x2_ref, g_ref, hr_ref, hn_ref, hq_ref, sk_ref, o_ref,
                     a_ref, b_ref, pq_ref, z_ref):
    L, te = v_ref.shape
    rc = min(L, 256)
    n_chunks = L // rc
    sign = (1 - 2 * (lax.broadcasted_iota(jnp.int32, (rc, te), 0) & 1)).astype(F32)

    def conv(z_in_ref, n, finish):
        zb = z_in_ref[...]
        a_ref[...] = jnp.dot(cs_ref[:, 0:L], zb, preferred_element_type=F32)
        b_ref[...] = jnp.dot(cs_ref[:, L:2 * L], zb, preferred_element_type=F32)

        def spectral_product(i, nyq):
            r0 = pl.multiple_of(i * rc, rc)
            a = a_ref[pl.ds(r0, rc), :]
            b = b_ref[pl.ds(r0, rc), :]
            hr = hr_ref[n, pl.ds(r0, rc), :]
            hn = hn_ref[n, pl.ds(r0, rc), :]
            pq_ref[pl.ds(r0, rc), :] = (a * hr - b * hn).astype(BF16)
            pq_ref[pl.ds(pl.multiple_of(L + r0, rc), rc), :] = (b * hr + a * hn).astype(BF16)
            zf = z_in_ref[pl.ds(r0, rc), :].astype(F32)
            return nyq + jnp.sum(sign * zf, axis=0, keepdims=True)

        nyq = lax.fori_loop(0, n_chunks, spectral_product, jnp.zeros((1, te), F32))
        nyq_term = sign * (nyq * hq_ref[n, 0:1, :])
        a_ref[...] = jnp.dot(cs_ref[...], pq_ref[...], preferred_element_type=F32)
        skip = sk_ref[n]

        def epilogue(i, carry):
            r0 = pl.multiple_of(i * rc, rc)
            zf = z_in_ref[pl.ds(r0, rc), :].astype(F32)
            finish(r0, a_ref[pl.ds(r0, rc), :] + nyq_term + skip * zf)
            return carry

        lax.fori_loop(0, n_chunks, epilogue, 0)

    def first(r0, y):
        z_ref[pl.ds(r0, rc), :] = (x1_ref[pl.ds(r0, rc), :].astype(F32) * y).astype(BF16)

    def second(r0, y):
        z2 = x2_ref[pl.ds(r0, rc), :].astype(F32) * y
        o_ref[pl.ds(r0, rc), :] = (z2 * g_ref[pl.ds(r0, rc), :].astype(F32)).astype(o_ref.dtype)

    conv(v_ref, 0, first)
    conv(z_ref, 1, second)


def _longconv(u3, g, cs, hr, hn, hq, skip):
    b, L, e3 = u3.shape
    e = e3 // 3
    te = 256
    ne = e // te
    seq = lambda off: pl.BlockSpec((None, L, te), lambda ei, bi: (bi, 0, off * ne + ei))
    return pl.pallas_call(
        _longconv_kernel,
        grid=(ne, b),
        in_specs=[_resident((L, 2 * L), lambda ei, bi: (0, 0)),
                  seq(0), seq(1), seq(2),
                  pl.BlockSpec((None, L, te), lambda ei, bi: (bi, 0, ei)),
                  _resident((HY_ORDER, L, te), lambda ei, bi: (0, 0, ei)),
                  _resident((HY_ORDER, L, te), lambda ei, bi: (0, 0, ei)),
                  pl.BlockSpec((HY_ORDER, 8, te), lambda ei, bi: (0, 0, ei)),
                  pl.BlockSpec((HY_ORDER, 1, te), lambda ei, bi: (0, 0, ei))],
        out_specs=pl.BlockSpec((None, L, te), lambda ei, bi: (bi, 0, ei)),
        out_shape=jax.ShapeDtypeStruct((b, L, e), BF16),
        scratch_shapes=[pltpu.VMEM((L, te), F32), pltpu.VMEM((L, te), F32),
                        pltpu.VMEM((2 * L, te), BF16), pltpu.VMEM((L, te), BF16)],
        compiler_params=_params("arbitrary", "arbitrary"),
        name="hyena_longconv",
    )(cs, u3, u3, u3, g, hr, hn, hq, skip.reshape(HY_ORDER, 1, e))


def _cfmid_kernel(cur_ref, prev_ref, next_ref, g_ref, w_ref, b_ref, lg_ref, lb_ref, o_ref,
                  pad_ref, cv_ref, *, n_tiles):
    i = pl.program_id(1)
    tl, e = cur_ref.shape
    pad_ref[0:CF_HALO, :] = jnp.where(i > 0, prev_ref[...].astype(F32), 0.0)
    pad_ref[CF_HALO:CF_HALO + tl, :] = cur_ref[...].astype(F32)
    pad_ref[CF_HALO + tl:2 * CF_HALO + tl, :] = jnp.where(i < n_tiles - 1, next_ref[...].astype(F32), 0.0)
    rc = 32
    lc = 512
    first = CF_HALO - (CF_KERNEL - 1) // 2

    def lane_body(li, carry):
        c0 = pl.multiple_of(li * lc, lc)
        bias = b_ref[:, pl.ds(c0, lc)]
        for r in range(tl // rc):
            acc = jnp.broadcast_to(bias, (rc, lc))
            for k in range(CF_KERNEL):
                acc = acc + pad_ref[pl.ds(r * rc + first + k, rc), pl.ds(c0, lc)] * w_ref[pl.ds(k, 1), pl.ds(c0, lc)]
            cv_ref[pl.ds(r * rc, rc), pl.ds(c0, lc)] = acc
        return carry

    lax.fori_loop(0, e // lc, lane_body, 0)
    cv = cv_ref[...]
    mu = jnp.mean(cv, axis=-1, keepdims=True)
    xc = cv - mu
    var = jnp.mean(xc * xc, axis=-1, keepdims=True)
    y = xc * lax.rsqrt(var + EPS) * lg_ref[...] + lb_ref[...]
    o_ref[...] = (_silu(y) * g_ref[...].astype(F32)).astype(o_ref.dtype)


def _cfmid(u, g, dw_w, dw_b, ln_g, ln_b):
    b, L, e = u.shape
    tl = min(L, 256)
    n_tiles = L // tl
    hb = tl // CF_HALO
    n_halo = L // CF_HALO
    wpad = jnp.pad(dw_w, ((0, 32 - CF_KERNEL), (0, 0)))
    vec = lambda: pl.BlockSpec((1, e), lambda bi, i: (0, 0))
    return pl.pallas_call(
        functools.partial(_cfmid_kernel, n_tiles=n_tiles),
        grid=(b, n_tiles),
        in_specs=[pl.BlockSpec((None, tl, e), lambda bi, i: (bi, i, 0)),
                  pl.BlockSpec((None, CF_HALO, e), lambda bi, i: (bi, jnp.maximum(i * hb - 1, 0), 0)),
                  pl.BlockSpec((None, CF_HALO, e), lambda bi, i: (bi, jnp.minimum((i + 1) * hb, n_halo - 1), 0)),
                  pl.BlockSpec((None, tl, e), lambda bi, i: (bi, i, 0)),
                  pl.BlockSpec((32, e), lambda bi, i: (0, 0)),
                  vec(), vec(), vec()],
        out_specs=pl.BlockSpec((None, tl, e), lambda bi, i: (bi, i, 0)),
        out_shape=jax.ShapeDtypeStruct((b, L, e), BF16),
        scratch_shapes=[pltpu.VMEM((tl + 2 * CF_HALO, e), F32), pltpu.VMEM((tl, e), F32)],
        compiler_params=_params("arbitrary", "arbitrary"),
        name="conformer_mid",
    )(u, u, u, g, wpad, dw_b.reshape(1, e), ln_g.reshape(1, e), ln_b.reshape(1, e))


def _retention_kernel(lg_ref, q_ref, k_ref, v_ref, g_ref, kc_ref, vc_ref, gg_ref, gb_ref, o_ref,
                      state_ref, oacc_ref, *, chunk):
    h = pl.program_id(1)
    L, dk = q_ref.shape
    dv = v_ref.shape[-1]
    n_lat = L // chunk
    n_ctx = kc_ref.shape[0] // chunk
    rowk = lax.broadcasted_iota(jnp.int32, (chunk, dk), 0).astype(F32)
    ri = lax.broadcasted_iota(jnp.int32, (chunk, chunk), 0)
    ci = lax.broadcasted_iota(jnp.int32, (chunk, chunk), 1)

    def run(direction):
        fwd = direction == 0
        lg = lg_ref[direction, h]
        if fwd:
            q_dec = jnp.exp(lg * (rowk + 1.0))
            k_dec = jnp.exp(lg * (chunk - 1.0 - rowk))
            rel = (ri - ci).astype(F32)
        else:
            q_dec = jnp.exp(lg * (chunk - rowk))
            k_dec = jnp.exp(lg * rowk)
            rel = (ci - ri).astype(F32)
        dmask = jnp.where(rel >= 0, jnp.exp(lg * jnp.maximum(rel, 0.0)), 0.0)
        c_dec = jnp.exp(jnp.full((1, dv), lg * chunk, F32))

        def state_update(st, kc, vc):
            kd = (kc.astype(F32) * k_dec).T.astype(BF16)
            return st * c_dec + jnp.dot(kd, vc, preferred_element_type=F32)

        state_ref[...] = jnp.zeros_like(state_ref)

        def ctx_body(j, carry):
            c = j if fwd else n_ctx - 1 - j
            r0 = pl.multiple_of(c * chunk, chunk)
            state_ref[...] = state_update(state_ref[...], kc_ref[pl.ds(r0, chunk), :], vc_ref[pl.ds(r0, chunk), :])
            return carry

        lax.fori_loop(0, n_ctx, ctx_body, 0)

        def lat_body(j, carry):
            c = j if fwd else n_lat - 1 - j
            r0 = pl.multiple_of(c * chunk, chunk)
            qc = q_ref[pl.ds(r0, chunk), :]
            kc = k_ref[pl.ds(r0, chunk), :]
            vc = v_ref[pl.ds(r0, chunk), :]
            st = state_ref[...]
            scores = lax.dot_general(qc, kc, (((1,), (1,)), ((), ())), preferred_element_type=F32) * dmask
            out = (jnp.dot(scores.astype(BF16), vc, preferred_element_type=F32)
                   + jnp.dot((qc.astype(F32) * q_dec).astype(BF16), st.astype(BF16), preferred_element_type=F32))
            state_ref[...] = state_update(st, kc, vc)
            if fwd:
                oacc_ref[pl.ds(r0, chunk), :] = out
            else:
                o = oacc_ref[pl.ds(r0, chunk), :] + out
                mu = jnp.mean(o, axis=-1, keepdims=True)
                oc = o - mu
                var = jnp.mean(oc * oc, axis=-1, keepdims=True)
                y = oc * lax.rsqrt(var + EPS) * gg_ref[...] + gb_ref[...]
                o_ref[pl.ds(r0, chunk), :] = (y * g_ref[pl.ds(r0, chunk), :].astype(F32)).astype(o_ref.dtype)
            return carry

        lax.fori_loop(0, n_lat, lat_body, 0)

    run(0)
    run(1)


def _retention(q, k, v, g, kc, vc, log_g, gn_g, gn_b):
    b, L, dqk = q.shape
    e = v.shape[-1]
    dk = dqk // RT_HEADS
    dv = e // RT_HEADS
    lc = kc.shape[1]
    head = lambda rows, width: pl.BlockSpec((None, rows, width), lambda bi, hi: (bi, 0, hi))
    return pl.pallas_call(
        functools.partial(_retention_kernel, chunk=RT_CHUNK),
        grid=(b, RT_HEADS),
        in_specs=[pl.BlockSpec(memory_space=pltpu.SMEM),
                  head(L, dk), head(L, dk), head(L, dv), head(L, dv), head(lc, dk), head(lc, dv),
                  pl.BlockSpec((1, dv), lambda bi, hi: (0, hi)),
                  pl.BlockSpec((1, dv), lambda bi, hi: (0, hi))],
        out_specs=head(L, dv),
        out_shape=jax.ShapeDtypeStruct((b, L, e), BF16),
        scratch_shapes=[pltpu.VMEM((dk, dv), F32), pltpu.VMEM((L, dv), F32)],
        compiler_params=_params("arbitrary", "arbitrary"),
        name="retention",
    )(log_g, q, k, v, g, kc, vc, gn_g.reshape(1, e), gn_b.reshape(1, e))


def _rope_tables(L, dk):
    quarter = dk // 4
    inv = ROPE_BASE ** (-jnp.arange(quarter, dtype=F32) / quarter)
    t = jnp.arange(L)
    ar = (t // GRID_W).astype(F32)[:, None] * inv
    ac = (t % GRID_W).astype(F32)[:, None] * inv
    cos = jnp.concatenate([jnp.cos(ar), jnp.cos(ar), jnp.cos(ac), jnp.cos(ac)], axis=-1)
    sin = jnp.concatenate([-jnp.sin(ar), jnp.sin(ar), -jnp.sin(ac), jnp.sin(ac)], axis=-1)
    return cos, sin


def _rms_kernel(x_ref, g_ref, o_ref):
    x = x_ref[...]
    o_ref[...] = x * lax.rsqrt(jnp.mean(x * x, axis=-1, keepdims=True) + EPS) * g_ref[...]


def _final_rmsnorm(x, g):
    b, L, d = x.shape
    tl = min(L, 512)
    return pl.pallas_call(
        _rms_kernel,
        grid=(b, L // tl),
        in_specs=[pl.BlockSpec((None, tl, d), lambda bi, i: (bi, i, 0)),
                  pl.BlockSpec((1, d), lambda bi, i: (0, 0))],
        out_specs=pl.BlockSpec((None, tl, d), lambda bi, i: (bi, i, 0)),
        out_shape=jax.ShapeDtypeStruct(x.shape, F32),
        compiler_params=_params("arbitrary", "arbitrary"),
        name="final_rmsnorm",
    )(x, g.reshape(1, d))


def _hyena_branch(h, w_in, conv_w, conv_b, spec, bases, skip):
    b, L, d = h.shape
    e = w_in.shape[1] // 4
    h2 = h.reshape(b * L, d)
    u3 = _proj(h2, w_in, 0, 3 * e, "conv3", seq_len=L, extra=(conv_w, conv_b)).reshape(b, L, 3 * e)
    g = _proj(h2, w_in, 3 * e, e, "silu", seq_len=L).reshape(b, L, e)
    return _longconv(u3, g, bases, spec[0], spec[1], spec[2], skip)


def _conformer_branch(h, w_in, dw_w, dw_b, ln_g, ln_b):
    b, L, d = h.shape
    e = w_in.shape[1] // 3
    h2 = h.reshape(b * L, d)
    u = _proj(h2, w_in, 0, e, "glu", seq_len=L, extra=(e,)).reshape(b, L, e)
    g = _proj(h2, w_in, 2 * e, e, "silu", seq_len=L).reshape(b, L, e)
    return _cfmid(u, g, dw_w, dw_b, ln_g, ln_b)


def _retention_branch(h_lat, h_ctx, w_in, decay_logit, gn_g, gn_b):
    b, L, d = h_lat.shape
    lc = h_ctx.shape[1]
    dqk = d
    e = (w_in.shape[1] - 2 * dqk) // 2
    dk = dqk // RT_HEADS
    k_scale = dk ** -0.5
    hl = h_lat.reshape(b * L, d)
    hc = h_ctx.reshape(b * lc, d)
    cos, sin = _rope_tables(L, dk)
    q = _proj(hl, w_in, 0, dqk, "rope", seq_len=L, extra=(cos, sin)).reshape(b, L, dqk)
    k = _proj(hl, w_in, dqk, dqk, "rope", seq_len=L, extra=(cos, sin), scale=k_scale).reshape(b, L, dqk)
    v = _proj(hl, w_in, 2 * dqk, e, "plain", seq_len=L).reshape(b, L, e)
    g = _proj(hl, w_in, 2 * dqk + e, e, "silu", seq_len=L).reshape(b, L, e)
    kc = _proj(hc, w_in, dqk, dqk, "scale", seq_len=lc, scale=k_scale).reshape(b, lc, dqk)
    vc = _proj(hc, w_in, 2 * dqk, e, "plain", seq_len=lc).reshape(b, lc, e)
    log_g = jax.nn.log_sigmoid(decay_logit.astype(F32))
    return _retention(q, k, v, g, kc, vc, log_g, gn_g, gn_b)


def kernel(x, c, ctx, c_ctx, ada_w, ada_b, norm_g, final_norm_g,
           hy_w_in, hy_conv_w, hy_conv_b, hy_f_w1, hy_f_b1, hy_f_fr1, hy_f_w2, hy_f_b2,
           hy_f_fr2, hy_f_w3, hy_skip, hy_w_out,
           cf_w_in, cf_dw_w, cf_dw_b, cf_ln_g, cf_ln_b, cf_w_out,
           rt_w_in, rt_decay_logit, rt_gn_g, rt_gn_b, rt_w_out):
    depth = ada_w.shape[0]
    b, L, d = x.shape
    lc = ctx.shape[1]
    mod = _modulation(c, c_ctx, ada_w, ada_b)
    lat_row = lambda bi: bi
    ctx_row = lambda bi: b
    bases = {}
    for i in range(depth):
        kind, j = i % N_MIXERS, i // N_MIXERS
        last = i == depth - 1
        need_ctx_out = (not last) and not (i == depth - 2 and (depth - 1) % N_MIXERS != 2)
        need_ctx_in = need_ctx_out or kind == 2
        h_lat = _normmod(x, norm_g[i], mod, i, lat_row)
        h_ctx = _normmod(ctx, norm_g[i], mod, i, ctx_row) if need_ctx_in else None
        u_ctx = None
        if kind == 0:
            e = hy_w_in.shape[2] // 4
            w_in = hy_w_in[j].astype(BF16)
            w_out = hy_w_out[j].astype(BF16)
            filt = (hy_f_w1[j], hy_f_b1[j], hy_f_fr1[j], hy_f_w2[j], hy_f_b2[j], hy_f_fr2[j], hy_f_w3[j])
            for seq in (L, lc):
                if seq not in bases:
                    bases[seq] = _dft_bases(seq)
            spec = _hyena_spectrum(L, e, bases[L], *filt)
            u_lat = _hyena_branch(h_lat, w_in, hy_conv_w[j], hy_conv_b[j], spec, bases[L], hy_skip[j])
            if need_ctx_out:
                spec_c = _hyena_spectrum(lc, e, bases[lc], *filt)
                u_ctx = _hyena_branch(h_ctx, w_in, hy_conv_w[j], hy_conv_b[j], spec_c, bases[lc], hy_skip[j])
        elif kind == 1:
            w_in = cf_w_in[j].astype(BF16)
            w_out = cf_w_out[j].astype(BF16)
            p = (cf_dw_w[j], cf_dw_b[j], cf_ln_g[j], cf_ln_b[j])
            u_lat = _conformer_branch(h_lat, w_in, *p)
            if need_ctx_out:
                u_ctx = _conformer_branch(h_ctx, w_in, *p)
        else:
            w_in = rt_w_in[j].astype(BF16)
            w_out = rt_w_out[j].astype(BF16)
            if need_ctx_out:
                raise NotImplementedError("context outputs of a retention layer are not consumed by this stack")
            u_lat = _retention_branch(h_lat, h_ctx, w_in, rt_decay_logit[j], rt_gn_g[j], rt_gn_b[j])
        x = _outproj(u_lat, w_out, x, mod, i, lat_row)
        if need_ctx_out:
            ctx = _outproj(u_ctx, w_out, ctx, mod, i, ctx_row)
    return _final_rmsnorm(x, final_norm_g)
```

```python
import functools
import math

import jax
import jax.numpy as jnp
from jax import lax
from jax.experimental import pallas as pl
from jax.experimental.pallas import tpu as pltpu

F32 = jnp.float32
BF16 = jnp.bfloat16

EPS = 1e-6
N_MIXERS = 3
GRID_W = 64
HY_ORDER = 2
HY_EMB = 33
HY_BANDS = (HY_EMB - 1) // 2
HY_FAST_PCT = 0.3
HY_SLOW_PCT = 1.5
HY_TARGET = 1e-2
HY_BLOCK = 512
CF_KERNEL = 31
CF_HALO = 16
RT_HEADS = 8
RT_CHUNK = 128
ROPE_BASE = 10000.0

V7X_VMEM_BYTES = 64 * 1024 * 1024
VMEM_LIMIT_BYTES = V7X_VMEM_BYTES - 8 * 1024 * 1024
LANE = 128
SUBLANES = 8


def _params(*semantics):
    return pltpu.CompilerParams(dimension_semantics=semantics, vmem_limit_bytes=VMEM_LIMIT_BYTES)


def _mod_kernel(c_ref, w_ref, b_ref, o_ref):
    c = c_ref[...]
    m = (c * jax.nn.sigmoid(c)).astype(BF16)
    o_ref[...] = jnp.dot(m, w_ref[...].astype(BF16), preferred_element_type=F32) + b_ref[...]


def _modulation(c, c_ctx, ada_w, ada_b):
    depth, d, n3 = ada_w.shape
    b = c.shape[0]
    rows = -(-(b + 1) // 8) * 8
    cc = jnp.zeros((rows, d), F32).at[:b].set(c).at[b].set(c_ctx)
    tn = 1536
    out = pl.pallas_call(
        _mod_kernel,
        grid=(depth, n3 // tn),
        in_specs=[pl.BlockSpec((rows, d), lambda l, n: (0, 0)),
                  pl.BlockSpec((None, d, tn), lambda l, n: (l, 0, n)),
                  pl.BlockSpec((None, 1, tn), lambda l, n: (l, 0, n))],
        out_specs=pl.BlockSpec((None, rows, tn), lambda l, n: (l, 0, n)),
        out_shape=jax.ShapeDtypeStruct((depth, rows, n3), F32),
        compiler_params=_params("arbitrary", "arbitrary"),
        name="adaln_modulation",
    )(cc, ada_w, ada_b.reshape(depth, 1, n3))
    return out.reshape(depth, rows, 1, n3)


def _normmod_kernel(x_ref, g_ref, sh_ref, sc_ref, o_ref):
    x = x_ref[...]
    y = x * lax.rsqrt(jnp.mean(x * x, axis=-1, keepdims=True) + EPS)
    o_ref[...] = ((y * g_ref[...]) * (1.0 + sc_ref[...]) + sh_ref[...]).astype(o_ref.dtype)


def _normmod(x, norm_g, mod, layer, row_of):
    b, L, d = x.shape
    tl = min(L, 512)
    return pl.pallas_call(
        _normmod_kernel,
        grid=(b, L // tl),
        in_specs=[pl.BlockSpec((None, tl, d), lambda bi, i: (bi, i, 0)),
                  pl.BlockSpec((1, d), lambda bi, i: (0, 0)),
                  pl.BlockSpec((None, None, 1, d), lambda bi, i: (layer, row_of(bi), 0, 0)),
                  pl.BlockSpec((None, None, 1, d), lambda bi, i: (layer, row_of(bi), 0, 1))],
        out_specs=pl.BlockSpec((None, tl, d), lambda bi, i: (bi, i, 0)),
        out_shape=jax.ShapeDtypeStruct((b, L, d), BF16),
        compiler_params=_params("arbitrary", "arbitrary"),
        name="normmod",
    )(x, norm_g.reshape(1, d), mod, mod)


def _silu(x):
    return x * jax.nn.sigmoid(x)


def _proj_kernel(*refs, mode, seq_len, scale):
    h_ref, w_ref = refs[0], refs[1]
    o_ref = refs[-1]
    acc = jnp.dot(h_ref[...], w_ref[...], preferred_element_type=F32)
    tm, tn = acc.shape
    if mode == "plain":
        y = acc
    elif mode == "scale":
        y = acc * scale
    elif mode == "silu":
        y = _silu(acc)
    elif mode == "glu":
        gate = jnp.dot(h_ref[...], refs[2][...], preferred_element_type=F32)
        y = acc * jax.nn.sigmoid(gate)
    elif mode == "conv3":
        cw = refs[2][...]
        row = lax.rem(lax.broadcasted_iota(jnp.int32, acc.shape, 0), seq_len)
        prev = jnp.where(row == 0, 0.0, pltpu.roll(acc, 1, axis=0))
        nxt = jnp.where(row == seq_len - 1, 0.0, pltpu.roll(acc, tm - 1, axis=0))
        y = prev * cw[0:1] + acc * cw[1:2] + nxt * cw[2:3] + refs[3][...]
    elif mode == "rope":
        cos, sin = refs[2][...], refs[3][...]
        dk = cos.shape[-1]
        quarter = dk // 4
        lane = lax.broadcasted_iota(jnp.int32, (tm, dk), 1)
        first = lax.rem(lane, 2 * quarter) < quarter
        parts = []
        for gi in range(tn // dk):
            xg = acc[:, gi * dk:(gi + 1) * dk]
            swapped = jnp.where(first, pltpu.roll(xg, dk - quarter, axis=1), pltpu.roll(xg, quarter, axis=1))
            parts.append((xg * cos + swapped * sin) * scale)
        y = parts[0] if len(parts) == 1 else jnp.concatenate(parts, axis=1)
    else:
        raise ValueError(mode)
    o_ref[...] = y.astype(o_ref.dtype)


def _proj(h2, w, col0, ncols, mode, *, seq_len, extra=(), scale=1.0, tn=512):
    m_rows, d = h2.shape
    n_seq = m_rows // seq_len
    k = 1
    if mode != "rope":
        for cand in range(1, n_seq + 1):
            if n_seq % cand == 0 and seq_len * cand <= 2048:
                k = cand
    tm = seq_len * k
    c0 = col0 // tn
    in_specs = [pl.BlockSpec((tm, d), lambda m, n: (m, 0)),
                pl.BlockSpec((d, tn), lambda m, n: (0, c0 + n))]
    args = [h2, w]
    if mode == "glu":
        gate_c0 = extra[0] // tn
        in_specs.append(pl.BlockSpec((d, tn), lambda m, n: (0, gate_c0 + n)))
        args.append(w)
    elif mode == "conv3":
        cw, cb = extra
        in_specs += [pl.BlockSpec((cw.shape[0], tn), lambda m, n: (0, n)),
                     pl.BlockSpec((1, tn), lambda m, n: (0, n))]
        args += [cw, cb.reshape(1, -1)]
    elif mode == "rope":
        cos, sin = extra
        dk = cos.shape[-1]
        in_specs += [pl.BlockSpec((tm, dk), lambda m, n: (0, 0)),
                     pl.BlockSpec((tm, dk), lambda m, n: (0, 0))]
        args += [cos, sin]
    return pl.pallas_call(
        functools.partial(_proj_kernel, mode=mode, seq_len=seq_len, scale=scale),
        grid=(m_rows // tm, ncols // tn),
        in_specs=in_specs,
        out_specs=pl.BlockSpec((tm, tn), lambda m, n: (m, n)),
        out_shape=jax.ShapeDtypeStruct((m_rows, ncols), BF16),
        compiler_params=_params("arbitrary", "arbitrary"),
        name="proj_" + mode,
    )(*args)


def _outproj_kernel(u_ref, w_ref, x_ref, gate_ref, o_ref):
    acc = jnp.dot(u_ref[...], w_ref[...], preferred_element_type=F32)
    o_ref[...] = x_ref[...] + gate_ref[...] * acc


def _outproj(u, w, x, mod, layer, row_of):
    b, L, e = u.shape
    d = x.shape[-1]
    tm = min(L, 1024)
    tn = 512
    gate_c0 = 2 * d // tn
    return pl.pallas_call(
        _outproj_kernel,
        grid=(b, L // tm, d // tn),
        in_specs=[pl.BlockSpec((None, tm, e), lambda bi, m, n: (bi, m, 0)),
                  pl.BlockSpec((e, tn), lambda bi, m, n: (0, n)),
                  pl.BlockSpec((None, tm, tn), lambda bi, m, n: (bi, m, n)),
                  pl.BlockSpec((None, None, 1, tn), lambda bi, m, n: (layer, row_of(bi), 0, gate_c0 + n))],
        out_specs=pl.BlockSpec((None, tm, tn), lambda bi, m, n: (bi, m, n)),
        out_shape=jax.ShapeDtypeStruct(x.shape, F32),
        compiler_params=_params("arbitrary", "arbitrary", "arbitrary"),
        name="outproj",
    )(u, w, x, mod)


def _dot3(a, b):
    a_hi = a.astype(BF16)
    a_lo = (a - a_hi.astype(F32)).astype(BF16)
    b_hi = b.astype(BF16)
    b_lo = (b - b_hi.astype(F32)).astype(BF16)
    d = lambda x, y: jnp.dot(x, y, preferred_element_type=F32)
    return d(a_hi, b_hi) + d(a_lo, b_hi) + d(a_hi, b_lo)


def _filter_kernel(feat_ref, featr_ref, w1_ref, b1_ref, fr1_ref, w2_ref, b2_ref, fr2_ref,
                   wf0_ref, wb0_ref, wf1_ref, wb1_ref, dl_ref, o_ref, hid_ref):
    hp = lax.Precision.HIGHEST
    L = feat_ref.shape[0]

    @pl.when(pl.program_id(0) == 0)
    def _():
        for side, f_ref in enumerate((feat_ref, featr_ref)):
            z1 = jnp.dot(f_ref[...], w1_ref[...], precision=hp, preferred_element_type=F32) + b1_ref[...]
            h1 = jnp.sin(fr1_ref[...] * z1)
            z2 = jnp.dot(h1, w2_ref[...], precision=hp, preferred_element_type=F32) + b2_ref[...]
            hid_ref[side] = jnp.sin(fr2_ref[...] * z2)

    te = dl_ref.shape[-1]
    row = lax.broadcasted_iota(jnp.int32, (L, te), 0)
    scale = 1.0 / (L - 1)
    win_f = jnp.exp(-(row.astype(F32) * scale) * dl_ref[...])
    win_b = jnp.exp(-((L - row).astype(F32) * scale) * dl_ref[...])
    h_f = hid_ref[0]
    h_b = hid_ref[1]
    for n, (wf_ref, wb_ref) in enumerate(((wf0_ref, wb0_ref), (wf1_ref, wb1_ref))):
        hb = _dot3(h_b, wb_ref[...]) * win_b
        o_ref[n, 0:L, :] = jnp.where(row == 0, 0.0, hb).astype(o_ref.dtype)
        o_ref[n, L:2 * L, :] = (_dot3(h_f, wf_ref[...]) * win_f).astype(o_ref.dtype)


def _spectrum_kernel(cs_ref, h_ref, hr_ref, hn_ref, hq_ref, *, blk):
    n_blocks = h_ref.shape[0] // blk
    ts = h_ref.shape[-1]
    inv_n = 1.0 / (2 * blk)
    row = lax.broadcasted_iota(jnp.int32, (blk, ts), 0)
    sign = (1 - 2 * (row & 1)).astype(F32)
    scale_r = jnp.where(row == 0, inv_n, 2.0 * inv_n)
    prev = None
    nyq_rows = []
    for bi in range(n_blocks):
        hb = h_ref[bi * blk:(bi + 1) * blk, :]
        fr = jnp.dot(cs_ref[:, 0:blk], hb, preferred_element_type=F32)
        fn = jnp.dot(cs_ref[:, blk:2 * blk], hb, preferred_element_type=F32)
        hf = hb.astype(F32)
        fq = jnp.sum(sign * hf, axis=0, keepdims=True)
        h0 = hf[0:1]
        if prev is not None:
            pfr, pfn, pfq, ph0 = prev
            hr_ref[bi - 1] = (fr + sign * (pfr - ph0)) * scale_r
            hn_ref[bi - 1] = (fn + sign * pfn) * (2.0 * inv_n)
            nyq_rows.append((fq + pfq - ph0) * inv_n)
        prev = (fr, fn, fq, h0)
    pad_rows = hq_ref.shape[0] - len(nyq_rows)
    if pad_rows:
        nyq_rows.append(jnp.zeros((pad_rows, ts), F32))
    hq_ref[...] = jnp.concatenate(nyq_rows, axis=0)


def _dft_bases(blk):
    k = jnp.arange(blk, dtype=jnp.int32)
    m = (k[:, None] * k[None, :]) % (2 * blk)
    ang = m.astype(F32) * (math.pi / blk)
    return jnp.concatenate([jnp.cos(ang), jnp.sin(ang)], axis=1).astype(BF16)


def _conv_block(L):
    return min(L, HY_BLOCK)


def _hyena_spectrum(L, e, cs, f_w1, f_b1, f_fr1, f_w2, f_b2, f_fr2, f_w3):
    fw = f_w1.shape[1]
    blk = _conv_block(L)
    n_lags = 2 * (L // blk) - 1
    t = jnp.linspace(0.0, 1.0, L, dtype=F32)[:, None]
    w = (2.0 * math.pi / L) * jnp.arange(L, dtype=F32)[:, None]
    f = jnp.linspace(1e-4, HY_BANDS - 1, HY_BANDS, dtype=F32)[None, :]
    feats = jnp.concatenate([t, jnp.cos(f * w), -jnp.sin(f * w)], axis=-1)
    feats = jnp.pad(feats, ((0, 0), (0, LANE - HY_EMB)))
    feats_rev = jnp.roll(feats[::-1], 1, axis=0)
    w1 = jnp.pad(f_w1, ((0, LANE - HY_EMB), (0, 0)))
    deltas = jnp.abs(jnp.linspace(math.log(HY_TARGET) / HY_SLOW_PCT,
                                  math.log(HY_TARGET) / HY_FAST_PCT, e, dtype=F32)).reshape(1, e)
    te = 512
    ne = e // te
    small = lambda shape: pl.BlockSpec(shape, lambda i: (0, 0))
    w3spec = lambda q: pl.BlockSpec((fw, te), lambda i: (0, q * ne + i))
    h_ext = pl.pallas_call(
        _filter_kernel,
        grid=(ne,),
        in_specs=[small((L, LANE)), small((L, LANE)), small((LANE, fw)), small((1, fw)), small((1, fw)),
                  small((fw, fw)), small((1, fw)), small((1, fw)),
                  w3spec(0), w3spec(1), w3spec(2), w3spec(3),
                  pl.BlockSpec((1, te), lambda i: (0, i))],
        out_specs=pl.BlockSpec((HY_ORDER, 2 * L, te), lambda i: (0, 0, i)),
        out_shape=jax.ShapeDtypeStruct((HY_ORDER, 2 * L, e), BF16),
        scratch_shapes=[pltpu.VMEM((2, L, fw), F32)],
        compiler_params=_params("arbitrary"),
        name="hyena_filter",
    )(feats, feats_rev, w1, f_b1.reshape(1, fw), f_fr1.reshape(1, fw), f_w2, f_b2.reshape(1, fw),
      f_fr2.reshape(1, fw), f_w3, f_w3, f_w3, f_w3, deltas)

    ts = 256
    nyq_rows = -(-n_lags // 8) * 8
    hr, hn, hq = pl.pallas_call(
        functools.partial(_spectrum_kernel, blk=blk),
        grid=(HY_ORDER, e // ts),
        in_specs=[pl.BlockSpec((blk, 2 * blk), lambda n, i: (0, 0)),
                  pl.BlockSpec((None, 2 * L, ts), lambda n, i: (n, 0, i))],
        out_specs=[pl.BlockSpec((None, n_lags, blk, ts), lambda n, i: (n, 0, 0, i)),
                   pl.BlockSpec((None, n_lags, blk, ts), lambda n, i: (n, 0, 0, i)),
                   pl.BlockSpec((None, nyq_rows, ts), lambda n, i: (n, 0, i))],
        out_shape=[jax.ShapeDtypeStruct((HY_ORDER, n_lags, blk, e), F32),
                   jax.ShapeDtypeStruct((HY_ORDER, n_lags, blk, e), F32),
                   jax.ShapeDtypeStruct((HY_ORDER, nyq_rows, e), F32)],
        compiler_params=_params("arbitrary", "arbitrary"),
        name="hyena_spectrum",
    )(cs, h_ext)
    return hr, hn, hq


def _longconv_kernel(cs_ref, v_ref, x1_ref, x2_ref, g_ref, hr_ref, hn_ref, hq_ref, sk_ref, o_ref,
                     a_ref, b_ref, p_ref, q_ref, z_ref, *, blk):
    L, te = v_ref.shape
    nb = L // blk
    gr = 16
    sign = (1 - 2 * (lax.broadcasted_iota(jnp.int32, (blk, te), 0) & 1)).astype(F32)
    dot = lambda x, y: jnp.dot(x, y, preferred_element_type=F32)

    def conv(z_in_ref, n, finish):
        nyq_in = []
        for j in range(nb):
            rows = slice(j * blk, (j + 1) * blk)
            zb = z_in_ref[rows, :]
            a_ref[rows, :] = dot(cs_ref[:, 0:blk], zb)
            b_ref[rows, :] = dot(cs_ref[:, blk:2 * blk], zb)
            nyq_in.append(jnp.sum(sign * zb.astype(F32), axis=0, keepdims=True))

        def pair_products(r, carry):
            r0 = pl.multiple_of(r * gr, gr)
            at = lambda j: pl.ds(pl.multiple_of(j * blk + r0, gr), gr)
            a = [a_ref[at(j), :] for j in range(nb)]
            b = [b_ref[at(j), :] for j in range(nb)]
            for i in range(nb):
                acc_p = acc_q = None
                for j in range(nb):
                    lag = i - j + nb - 1
                    hr = hr_ref[n, lag, pl.ds(r0, gr), :]
                    hn = hn_ref[n, lag, pl.ds(r0, gr), :]
                    tp = a[j] * hr - b[j] * hn
                    tq = b[j] * hr + a[j] * hn
                    acc_p = tp if acc_p is None else acc_p + tp
                    acc_q = tq if acc_q is None else acc_q + tq
                p_ref[at(i), :] = acc_p.astype(BF16)
                q_ref[at(i), :] = acc_q.astype(BF16)
            return carry

        lax.fori_loop(0, blk // gr, pair_products, 0)
        skip = sk_ref[n]
        for i in range(nb):
            rows = slice(i * blk, (i + 1) * blk)
            nyq = None
            for j in range(nb):
                lag = i - j + nb - 1
                term = nyq_in[j] * hq_ref[n, lag:lag + 1, :]
                nyq = term if nyq is None else nyq + term
            y = dot(cs_ref[:, 0:blk], p_ref[rows, :]) + dot(cs_ref[:, blk:2 * blk], q_ref[rows, :]) + sign * nyq
            finish(rows, y + skip * z_in_ref[rows, :].astype(F32))

    def first(rows, y):
        z_ref[rows, :] = (x1_ref[rows, :].astype(F32) * y).astype(BF16)

    def second(rows, y):
        o_ref[rows, :] = (x2_ref[rows, :].astype(F32) * y * g_ref[rows, :].astype(F32)).astype(o_ref.dtype)

    conv(v_ref, 0, first)
    conv(z_ref, 1, second)


def _longconv(u3, g, cs, hr, hn, hq, skip):
    b, L, e3 = u3.shape
    e = e3 // 3
    blk = cs.shape[0]
    n_lags = hr.shape[1]
    te = 256
    ne = e // te
    seq = lambda off: pl.BlockSpec((None, L, te), lambda ei, bi: (bi, 0, off * ne + ei))
    spec = lambda: pl.BlockSpec((HY_ORDER, n_lags, blk, te), lambda ei, bi: (0, 0, 0, ei))
    return pl.pallas_call(
        functools.partial(_longconv_kernel, blk=blk),
        grid=(ne, b),
        in_specs=[pl.BlockSpec((blk, 2 * blk), lambda ei, bi: (0, 0)),
                  seq(0), seq(1), seq(2),
                  pl.BlockSpec((None, L, te), lambda ei, bi: (bi, 0, ei)),
                  spec(), spec(),
                  pl.BlockSpec((HY_ORDER, hq.shape[1], te), lambda ei, bi: (0, 0, ei)),
                  pl.BlockSpec((HY_ORDER, 1, te), lambda ei, bi: (0, 0, ei))],
        out_specs=pl.BlockSpec((None, L, te), lambda ei, bi: (bi, 0, ei)),
        out_shape=jax.ShapeDtypeStruct((b, L, e), BF16),
        scratch_shapes=[pltpu.VMEM((L, te), F32), pltpu.VMEM((L, te), F32),
                        pltpu.VMEM((L, te), BF16), pltpu.VMEM((L, te), BF16), pltpu.VMEM((L, te), BF16)],
        compiler_params=_params("arbitrary", "arbitrary"),
        name="hyena_longconv",
    )(cs, u3, u3, u3, g, hr, hn, hq, skip.reshape(HY_ORDER, 1, e))


def _cfmid_kernel(cur_ref, prev_ref, next_ref, g_ref, w_ref, b_ref, lg_ref, lb_ref, o_ref,
                  pad_ref, sh_ref, cv_ref, *, n_tiles):
    i = pl.program_id(1)
    tl, e = cur_ref.shape
    pad_ref[0:CF_HALO, :] = jnp.where(i > 0, prev_ref[...].astype(F32), 0.0)
    pad_ref[CF_HALO:CF_HALO + tl, :] = cur_ref[...].astype(F32)
    pad_ref[CF_HALO + tl:2 * CF_HALO + tl, :] = jnp.where(i < n_tiles - 1, next_ref[...].astype(F32), 0.0)
    rc = 32
    lc = sh_ref.shape[-1]
    sh_rows = sh_ref.shape[1]
    first = CF_HALO - (CF_KERNEL - 1) // 2

    def lane_body(li, carry):
        c0 = pl.multiple_of(li * lc, lc)
        lanes = pl.ds(c0, lc)
        for s in range(1, SUBLANES):
            sh_ref[s - 1] = pad_ref[pl.ds(s, sh_rows), lanes]
        bias = b_ref[:, lanes]
        for r in range(tl // rc):
            acc = jnp.broadcast_to(bias, (rc, lc))
            for k in range(CF_KERNEL):
                phase = (first + k) % SUBLANES
                base = r * rc + first + k - phase
                if phase == 0:
                    tap = pad_ref[pl.ds(base, rc), lanes]
                else:
                    tap = sh_ref[phase - 1, pl.ds(base, rc), :]
                acc = acc + tap * w_ref[pl.ds(k, 1), lanes]
            cv_ref[pl.ds(r * rc, rc), lanes] = acc
        return carry

    lax.fori_loop(0, e // lc, lane_body, 0)
    ng = min(tl, 128)

    def norm_body(ri, carry):
        rows = pl.ds(pl.multiple_of(ri * ng, ng), ng)
        cv = cv_ref[rows, :]
        mu = jnp.mean(cv, axis=-1, keepdims=True)
        xc = cv - mu
        var = jnp.mean(xc * xc, axis=-1, keepdims=True)
        y = xc * lax.rsqrt(var + EPS) * lg_ref[...] + lb_ref[...]
        o_ref[rows, :] = (_silu(y) * g_ref[rows, :].astype(F32)).astype(o_ref.dtype)
        return carry

    lax.fori_loop(0, tl // ng, norm_body, 0)


def _cfmid(u, g, dw_w, dw_b, ln_g, ln_b):
    b, L, e = u.shape
    tl = min(L, 256)
    n_tiles = L // tl
    hb = tl // CF_HALO
    n_halo = L // CF_HALO
    wpad = jnp.pad(dw_w, ((0, 32 - CF_KERNEL), (0, 0)))
    vec = lambda: pl.BlockSpec((1, e), lambda bi, i: (0, 0))
    return pl.pallas_call(
        functools.partial(_cfmid_kernel, n_tiles=n_tiles),
        grid=(b, n_tiles),
        in_specs=[pl.BlockSpec((None, tl, e), lambda bi, i: (bi, i, 0)),
                  pl.BlockSpec((None, CF_HALO, e), lambda bi, i: (bi, jnp.maximum(i * hb - 1, 0), 0)),
                  pl.BlockSpec((None, CF_HALO, e), lambda bi, i: (bi, jnp.minimum((i + 1) * hb, n_halo - 1), 0)),
                  pl.BlockSpec((None, tl, e), lambda bi, i: (bi, i, 0)),
                  pl.BlockSpec((32, e), lambda bi, i: (0, 0)),
                  vec(), vec(), vec()],
        out_specs=pl.BlockSpec((None, tl, e), lambda bi, i: (bi, i, 0)),
        out_shape=jax.ShapeDtypeStruct((b, L, e), BF16),
        scratch_shapes=[pltpu.VMEM((tl + 2 * CF_HALO, e), F32),
                        pltpu.VMEM((SUBLANES - 1, tl + 2 * CF_HALO - SUBLANES, 512), F32),
                        pltpu.VMEM((tl, e), F32)],
        compiler_params=_params("arbitrary", "arbitrary"),
        name="conformer_mid",
    )(u, u, u, g, wpad, dw_b.reshape(1, e), ln_g.reshape(1, e), ln_b.reshape(1, e))


def _retention_kernel(lg_ref, q_ref, k_ref, v_ref, g_ref, kc_ref, vc_ref, gg_ref, gb_ref, o_ref,
                      state_ref, oacc_ref, *, chunk):
    h = pl.program_id(1)
    L, dk = q_ref.shape
    dv = v_ref.shape[-1]
    n_lat = L // chunk
    n_ctx = kc_ref.shape[0] // chunk
    rowk = lax.broadcasted_iota(jnp.int32, (chunk, dk), 0).astype(F32)
    ri = lax.broadcasted_iota(jnp.int32, (chunk, chunk), 0)
    ci = lax.broadcasted_iota(jnp.int32, (chunk, chunk), 1)

    def run(direction):
        fwd = direction == 0
        lg = lg_ref[direction, h]
        if fwd:
            q_dec = jnp.exp(lg * (rowk + 1.0))
            k_dec = jnp.exp(lg * (chunk - 1.0 - rowk))
            rel = (ri - ci).astype(F32)
        else:
            q_dec = jnp.exp(lg * (chunk - rowk))
            k_dec = jnp.exp(lg * rowk)
            rel = (ci - ri).astype(F32)
        dmask = jnp.where(rel >= 0, jnp.exp(lg * jnp.maximum(rel, 0.0)), 0.0)
        c_dec = jnp.exp(jnp.full((1, dv), lg * chunk, F32))

        def state_update(st, kc, vc):
            kd = (kc.astype(F32) * k_dec).T.astype(BF16)
            return st * c_dec + jnp.dot(kd, vc, preferred_element_type=F32)

        state_ref[...] = jnp.zeros_like(state_ref)

        def ctx_body(j, carry):
            c = j if fwd else n_ctx - 1 - j
            r0 = pl.multiple_of(c * chunk, chunk)
            state_ref[...] = state_update(state_ref[...], kc_ref[pl.ds(r0, chunk), :], vc_ref[pl.ds(r0, chunk), :])
            return carry

        lax.fori_loop(0, n_ctx, ctx_body, 0)

        def lat_body(j, carry):
            c = j if fwd else n_lat - 1 - j
            r0 = pl.multiple_of(c * chunk, chunk)
            qc = q_ref[pl.ds(r0, chunk), :]
            kc = k_ref[pl.ds(r0, chunk), :]
            vc = v_ref[pl.ds(r0, chunk), :]
            st = state_ref[...]
            scores = lax.dot_general(qc, kc, (((1,), (1,)), ((), ())), preferred_element_type=F32) * dmask
            out = (jnp.dot(scores.astype(BF16), vc, preferred_element_type=F32)
                   + jnp.dot((qc.astype(F32) * q_dec).astype(BF16), st.astype(BF16), preferred_element_type=F32))
            state_ref[...] = state_update(st, kc, vc)
            if fwd:
                oacc_ref[pl.ds(r0, chunk), :] = out
            else:
                o = oacc_ref[pl.ds(r0, chunk), :] + out
                mu = jnp.mean(o, axis=-1, keepdims=True)
                oc = o - mu
                var = jnp.mean(oc * oc, axis=-1, keepdims=True)
                y = oc * lax.rsqrt(var + EPS) * gg_ref[...] + gb_ref[...]
                o_ref[pl.ds(r0, chunk), :] = (y * g_ref[pl.ds(r0, chunk), :].astype(F32)).astype(o_ref.dtype)
            return carry

        lax.fori_loop(0, n_lat, lat_body, 0)

    run(0)
    run(1)


def _retention(q, k, v, g, kc, vc, log_g, gn_g, gn_b):
    b, L, dqk = q.shape
    e = v.shape[-1]
    dk = dqk // RT_HEADS
    dv = e // RT_HEADS
    lc = kc.shape[1]
    chunk = 2 * RT_CHUNK if (L % (2 * RT_CHUNK) == 0 and lc % (2 * RT_CHUNK) == 0) else RT_CHUNK
    head = lambda rows, width: pl.BlockSpec((None, rows, width), lambda bi, hi: (bi, 0, hi))
    return pl.pallas_call(
        functools.partial(_retention_kernel, chunk=chunk),
        grid=(b, RT_HEADS),
        in_specs=[pl.BlockSpec(memory_space=pltpu.SMEM),
                  head(L, dk), head(L, dk), head(L, dv), head(L, dv), head(lc, dk), head(lc, dv),
                  pl.BlockSpec((1, dv), lambda bi, hi: (0, hi)),
                  pl.BlockSpec((1, dv), lambda bi, hi: (0, hi))],
        out_specs=head(L, dv),
        out_shape=jax.ShapeDtypeStruct((b, L, e), BF16),
        scratch_shapes=[pltpu.VMEM((dk, dv), F32), pltpu.VMEM((L, dv), F32)],
        compiler_params=_params("arbitrary", "arbitrary"),
        name="retention",
    )(log_g, q, k, v, g, kc, vc, gn_g.reshape(1, e), gn_b.reshape(1, e))


def _rope_tables(L, dk):
    quarter = dk // 4
    inv = ROPE_BASE ** (-jnp.arange(quarter, dtype=F32) / quarter)
    t = jnp.arange(L)
    ar = (t // GRID_W).astype(F32)[:, None] * inv
    ac = (t % GRID_W).astype(F32)[:, None] * inv
    cos = jnp.concatenate([jnp.cos(ar), jnp.cos(ar), jnp.cos(ac), jnp.cos(ac)], axis=-1)
    sin = jnp.concatenate([-jnp.sin(ar), jnp.sin(ar), -jnp.sin(ac), jnp.sin(ac)], axis=-1)
    return cos, sin


def _rms_kernel(x_ref, g_ref, o_ref):
    x = x_ref[...]
    o_ref[...] = x * lax.rsqrt(jnp.mean(x * x, axis=-1, keepdims=True) + EPS) * g_ref[...]


def _final_rmsnorm(x, g):
    b, L, d = x.shape
    tl = min(L, 512)
    return pl.pallas_call(
        _rms_kernel,
        grid=(b, L // tl),
        in_specs=[pl.BlockSpec((None, tl, d), lambda bi, i: (bi, i, 0)),
                  pl.BlockSpec((1, d), lambda bi, i: (0, 0))],
        out_specs=pl.BlockSpec((None, tl, d), lambda bi, i: (bi, i, 0)),
        out_shape=jax.ShapeDtypeStruct(x.shape, F32),
        compiler_params=_params("arbitrary", "arbitrary"),
        name="final_rmsnorm",
    )(x, g.reshape(1, d))


def _hyena_branch(h, w_in, conv_w, conv_b, spec, bases, skip):
    b, L, d = h.shape
    e = w_in.shape[1] // 4
    h2 = h.reshape(b * L, d)
    u3 = _proj(h2, w_in, 0, 3 * e, "conv3", seq_len=L, extra=(conv_w, conv_b)).reshape(b, L, 3 * e)
    g = _proj(h2, w_in, 3 * e, e, "silu", seq_len=L).reshape(b, L, e)
    return _longconv(u3, g, bases, spec[0], spec[1], spec[2], skip)


def _conformer_branch(h, w_in, dw_w, dw_b, ln_g, ln_b):
    b, L, d = h.shape
    e = w_in.shape[1] // 3
    h2 = h.reshape(b * L, d)
    u = _proj(h2, w_in, 0, e, "glu", seq_len=L, extra=(e,)).reshape(b, L, e)
    g = _proj(h2, w_in, 2 * e, e, "silu", seq_len=L).reshape(b, L, e)
    return _cfmid(u, g, dw_w, dw_b, ln_g, ln_b)


def _retention_branch(h_lat, h_ctx, w_in, decay_logit, gn_g, gn_b):
    b, L, d = h_lat.shape
    lc = h_ctx.shape[1]
    dqk = d
    e = (w_in.shape[1] - 2 * dqk) // 2
    dk = dqk // RT_HEADS
    k_scale = dk ** -0.5
    hl = h_lat.reshape(b * L, d)
    hc = h_ctx.reshape(b * lc, d)
    cos, sin = _rope_tables(L, dk)
    q = _proj(hl, w_in, 0, dqk, "rope", seq_len=L, extra=(cos, sin)).reshape(b, L, dqk)
    k = _proj(hl, w_in, dqk, dqk, "rope", seq_len=L, extra=(cos, sin), scale=k_scale).reshape(b, L, dqk)
    v = _proj(hl, w_in, 2 * dqk, e, "plain", seq_len=L).reshape(b, L, e)
    g = _proj(hl, w_in, 2 * dqk + e, e, "silu", seq_len=L).reshape(b, L, e)
    kc = _proj(hc, w_in, dqk, dqk, "scale", seq_len=lc, scale=k_scale).reshape(b, lc, dqk)
    vc = _proj(hc, w_in, 2 * dqk, e, "plain", seq_len=lc).reshape(b, lc, e)
    log_g = jax.nn.log_sigmoid(decay_logit.astype(F32))
    return _retention(q, k, v, g, kc, vc, log_g, gn_g, gn_b)


def kernel(x, c, ctx, c_ctx, ada_w, ada_b, norm_g, final_norm_g,
           hy_w_in, hy_conv_w, hy_conv_b, hy_f_w1, hy_f_b1, hy_f_fr1, hy_f_w2, hy_f_b2,
           hy_f_fr2, hy_f_w3, hy_skip, hy_w_out,
           cf_w_in, cf_dw_w, cf_dw_b, cf_ln_g, cf_ln_b, cf_w_out,
           rt_w_in, rt_decay_logit, rt_gn_g, rt_gn_b, rt_w_out):
    depth = ada_w.shape[0]
    b, L, d = x.shape
    lc = ctx.shape[1]
    mod = _modulation(c, c_ctx, ada_w, ada_b)
    lat_row = lambda bi: bi
    ctx_row = lambda bi: b
    bases = {}
    for i in range(depth):
        kind, j = i % N_MIXERS, i // N_MIXERS
        last = i == depth - 1
        need_ctx_out = (not last) and not (i == depth - 2 and (depth - 1) % N_MIXERS != 2)
        need_ctx_in = need_ctx_out or kind == 2
        h_lat = _normmod(x, norm_g[i], mod, i, lat_row)
        h_ctx = _normmod(ctx, norm_g[i], mod, i, ctx_row) if need_ctx_in else None
        u_ctx = None
        if kind == 0:
            e = hy_w_in.shape[2] // 4
            w_in = hy_w_in[j].astype(BF16)
            w_out = hy_w_out[j].astype(BF16)
            filt = (hy_f_w1[j], hy_f_b1[j], hy_f_fr1[j], hy_f_w2[j], hy_f_b2[j], hy_f_fr2[j], hy_f_w3[j])
            for seq in (L, lc):
                if seq not in bases:
                    bases[seq] = _dft_bases(_conv_block(seq))
            spec = _hyena_spectrum(L, e, bases[L], *filt)
            u_lat = _hyena_branch(h_lat, w_in, hy_conv_w[j], hy_conv_b[j], spec, bases[L], hy_skip[j])
            if need_ctx_out:
                spec_c = _hyena_spectrum(lc, e, bases[lc], *filt)
                u_ctx = _hyena_branch(h_ctx, w_in, hy_conv_w[j], hy_conv_b[j], spec_c, bases[lc], hy_skip[j])
        elif kind == 1:
            w_in = cf_w_in[j].astype(BF16)
            w_out = cf_w_out[j].astype(BF16)
            p = (cf_dw_w[j], cf_dw_b[j], cf_ln_g[j], cf_ln_b[j])
            u_lat = _conformer_branch(h_lat, w_in, *p)
            if need_ctx_out:
                u_ctx = _conformer_branch(h_ctx, w_in, *p)
        else:
            w_in = rt_w_in[j].astype(BF16)
            w_out = rt_w_out[j].astype(BF16)
            if need_ctx_out:
                raise NotImplementedError("context outputs of a retention layer are not consumed by this stack")
            u_lat = _retention_branch(h_lat, h_ctx, w_in, rt_decay_logit[j], rt_gn_g[j], rt_gn_b[j])
        x = _outproj(u_lat, w_out, x, mod, i, lat_row)
        if need_ctx_out:
            ctx = _outproj(u_ctx, w_out, ctx, mod, i, ctx_row)
    return _final_rmsnorm(x, final_norm_g)
```

```python
import functools
import math

import jax
import jax.numpy as jnp
from jax import lax
from jax.experimental import pallas as pl
from jax.experimental.pallas import tpu as pltpu

F32 = jnp.float32
BF16 = jnp.bfloat16

EPS = 1e-6
N_MIXERS = 3
GRID_W = 64
HY_ORDER = 2
HY_EMB = 33
HY_BANDS = (HY_EMB - 1) // 2
HY_FAST_PCT = 0.3
HY_SLOW_PCT = 1.5
HY_TARGET = 1e-2
HY_BLOCK = 512
CF_KERNEL = 31
CF_HALO = 16
RT_HEADS = 8
RT_CHUNK = 128
ROPE_BASE = 10000.0

V7X_VMEM_BYTES = 64 * 1024 * 1024
VMEM_LIMIT_BYTES = V7X_VMEM_BYTES - 8 * 1024 * 1024
LANE = 128
SUBLANES = 8
MXU_ROWS = 256


def _params(*semantics):
    return pltpu.CompilerParams(dimension_semantics=semantics, vmem_limit_bytes=VMEM_LIMIT_BYTES)


def _mod_kernel(c_ref, w_ref, b_ref, o_ref):
    c = c_ref[...]
    m = (c * jax.nn.sigmoid(c)).astype(BF16)
    o_ref[...] = jnp.dot(m, w_ref[...].astype(BF16), preferred_element_type=F32) + b_ref[...]


def _modulation(c, c_ctx, ada_w, ada_b):
    depth, d, n3 = ada_w.shape
    b = c.shape[0]
    rows = -(-(b + 1) // 8) * 8
    cc = jnp.zeros((rows, d), F32).at[:b].set(c).at[b].set(c_ctx)
    tn = 1536
    out = pl.pallas_call(
        _mod_kernel,
        grid=(depth, n3 // tn),
        in_specs=[pl.BlockSpec((rows, d), lambda l, n: (0, 0)),
                  pl.BlockSpec((None, d, tn), lambda l, n: (l, 0, n)),
                  pl.BlockSpec((None, 1, tn), lambda l, n: (l, 0, n))],
        out_specs=pl.BlockSpec((None, rows, tn), lambda l, n: (l, 0, n)),
        out_shape=jax.ShapeDtypeStruct((depth, rows, n3), F32),
        compiler_params=_params("arbitrary", "arbitrary"),
        name="adaln_modulation",
    )(cc, ada_w, ada_b.reshape(depth, 1, n3))
    return out.reshape(depth, rows, 1, n3)


def _normmod_kernel(x_ref, g_ref, sh_ref, sc_ref, o_ref):
    x = x_ref[...]
    y = x * lax.rsqrt(jnp.mean(x * x, axis=-1, keepdims=True) + EPS)
    o_ref[...] = ((y * g_ref[...]) * (1.0 + sc_ref[...]) + sh_ref[...]).astype(o_ref.dtype)


def _normmod(x, norm_g, mod, layer, row_of):
    b, L, d = x.shape
    tl = min(L, 512)
    return pl.pallas_call(
        _normmod_kernel,
        grid=(b, L // tl),
        in_specs=[pl.BlockSpec((None, tl, d), lambda bi, i: (bi, i, 0)),
                  pl.BlockSpec((1, d), lambda bi, i: (0, 0)),
                  pl.BlockSpec((None, None, 1, d), lambda bi, i: (layer, row_of(bi), 0, 0)),
                  pl.BlockSpec((None, None, 1, d), lambda bi, i: (layer, row_of(bi), 0, 1))],
        out_specs=pl.BlockSpec((None, tl, d), lambda bi, i: (bi, i, 0)),
        out_shape=jax.ShapeDtypeStruct((b, L, d), BF16),
        compiler_params=_params("arbitrary", "arbitrary"),
        name="normmod",
    )(x, norm_g.reshape(1, d), mod, mod)


def _silu(x):
    return x * jax.nn.sigmoid(x)


def _proj_kernel(*refs, mode, seq_len, scale, mc):
    h_ref, w_ref = refs[0], refs[1]
    tm = h_ref.shape[0]
    tn = w_ref.shape[1]
    n_chunks = tm // mc
    dot = lambda x, y: jnp.dot(x, y, preferred_element_type=F32)

    if mode == "conv3":
        cw_ref, cb_ref, o_ref, acc_ref = refs[2:6]
        halo = jnp.zeros((SUBLANES, tn), F32)
        acc_ref[0:SUBLANES, :] = halo
        acc_ref[SUBLANES + tm:2 * SUBLANES + tm, :] = halo
        cw = cw_ref[...]
        cb = cb_ref[...]
        local = lax.broadcasted_iota(jnp.int32, (mc, tn), 0)

        def conv_rows(i):
            r0 = SUBLANES + i * mc
            prev = acc_ref[r0 - 1:r0 - 1 + mc, :]
            nxt = acc_ref[r0 + 1:r0 + 1 + mc, :]
            if (i * mc) % seq_len == 0:
                prev = jnp.where(local == 0, 0.0, prev)
            if ((i + 1) * mc) % seq_len == 0:
                nxt = jnp.where(local == mc - 1, 0.0, nxt)
            y = prev * cw[0:1] + acc_ref[r0:r0 + mc, :] * cw[1:2] + nxt * cw[2:3] + cb
            o_ref[i * mc:(i + 1) * mc, :] = y.astype(o_ref.dtype)

        for i in range(n_chunks):
            acc_ref[SUBLANES + i * mc:SUBLANES + (i + 1) * mc, :] = dot(h_ref[i * mc:(i + 1) * mc, :], w_ref[...])
            if i > 0:
                conv_rows(i - 1)
        conv_rows(n_chunks - 1)
        return

    o_ref = refs[-1]
    for i in range(n_chunks):
        rows = slice(i * mc, (i + 1) * mc)
        acc = dot(h_ref[rows, :], w_ref[...])
        if mode == "plain":
            y = acc
        elif mode == "scale":
            y = acc * scale
        elif mode == "silu":
            y = _silu(acc)
        elif mode == "glu":
            y = acc * jax.nn.sigmoid(dot(h_ref[rows, :], refs[2][...]))
        elif mode == "rope":
            cos, sin = refs[2][rows, :], refs[3][rows, :]
            dk = cos.shape[-1]
            quarter = dk // 4
            lane = lax.broadcasted_iota(jnp.int32, (mc, dk), 1)
            first = lax.rem(lane, 2 * quarter) < quarter
            parts = []
            for gi in range(tn // dk):
                xg = acc[:, gi * dk:(gi + 1) * dk]
                swapped = jnp.where(first, pltpu.roll(xg, dk - quarter, axis=1), pltpu.roll(xg, quarter, axis=1))
                parts.append((xg * cos + swapped * sin) * scale)
            y = parts[0] if len(parts) == 1 else jnp.concatenate(parts, axis=1)
        else:
            raise ValueError(mode)
        o_ref[rows, :] = y.astype(o_ref.dtype)


def _proj(h2, w, col0, ncols, mode, *, seq_len, extra=(), scale=1.0):
    m_rows, d = h2.shape
    n_seq = m_rows // seq_len
    k = 1
    if mode != "rope":
        for cand in range(1, n_seq + 1):
            if n_seq % cand == 0 and seq_len * cand <= 2048:
                k = cand
    tm = seq_len * k
    mc = math.gcd(seq_len, MXU_ROWS)
    col_starts = [col0] + ([extra[0]] if mode == "glu" else [])
    tn = 1024 if ncols % 1024 == 0 and all(c % 1024 == 0 for c in col_starts) else 512
    c0 = col0 // tn
    scratch = [pltpu.VMEM((tm + 2 * SUBLANES, tn), F32)] if mode == "conv3" else []
    in_specs = [pl.BlockSpec((tm, d), lambda m, n: (m, 0)),
                pl.BlockSpec((d, tn), lambda m, n: (0, c0 + n))]
    args = [h2, w]
    if mode == "glu":
        gate_c0 = extra[0] // tn
        in_specs.append(pl.BlockSpec((d, tn), lambda m, n: (0, gate_c0 + n)))
        args.append(w)
    elif mode == "conv3":
        cw, cb = extra
        in_specs += [pl.BlockSpec((cw.shape[0], tn), lambda m, n: (0, n)),
                     pl.BlockSpec((1, tn), lambda m, n: (0, n))]
        args += [cw, cb.reshape(1, -1)]
    elif mode == "rope":
        cos, sin = extra
        dk = cos.shape[-1]
        in_specs += [pl.BlockSpec((tm, dk), lambda m, n: (0, 0)),
                     pl.BlockSpec((tm, dk), lambda m, n: (0, 0))]
        args += [cos, sin]
    return pl.pallas_call(
        functools.partial(_proj_kernel, mode=mode, seq_len=seq_len, scale=scale, mc=mc),
        grid=(m_rows // tm, ncols // tn),
        in_specs=in_specs,
        out_specs=pl.BlockSpec((tm, tn), lambda m, n: (m, n)),
        out_shape=jax.ShapeDtypeStruct((m_rows, ncols), BF16),
        scratch_shapes=scratch,
        compiler_params=_params("arbitrary", "arbitrary"),
        name="proj_" + mode,
    )(*args)


def _outproj_kernel(u_ref, w_ref, x_ref, gate_ref, g_ref, sh_ref, sc_ref, *out_refs, final, mc):
    tm = u_ref.shape[0]
    for i in range(tm // mc):
        rows = slice(i * mc, (i + 1) * mc)
        acc = jnp.dot(u_ref[rows, :], w_ref[...], preferred_element_type=F32)
        xn = x_ref[rows, :] + gate_ref[...] * acc
        y = xn * lax.rsqrt(jnp.mean(xn * xn, axis=-1, keepdims=True) + EPS) * g_ref[...]
        if final:
            out_refs[0][rows, :] = y
        else:
            out_refs[0][rows, :] = xn
            out_refs[1][rows, :] = (y * (1.0 + sc_ref[...]) + sh_ref[...]).astype(out_refs[1].dtype)


def _outproj(u, w, x, mod, layer, row_of, norm_g, next_layer):
    b, L, e = u.shape
    d = x.shape[-1]
    tm = min(L, 512)
    mc = math.gcd(tm, MXU_ROWS)
    final = next_layer is None
    mod_layer = layer if final else next_layer
    row_block = lambda: pl.BlockSpec((None, tm, d), lambda bi, m: (bi, m, 0))
    mod_block = lambda lyr, part: pl.BlockSpec((None, None, 1, d), lambda bi, m: (lyr, row_of(bi), 0, part))
    out_specs = [row_block()] if final else [row_block(), row_block()]
    out_shape = [jax.ShapeDtypeStruct(x.shape, F32)] + ([] if final else [jax.ShapeDtypeStruct(x.shape, BF16)])
    outs = pl.pallas_call(
        functools.partial(_outproj_kernel, final=final, mc=mc),
        grid=(b, L // tm),
        in_specs=[pl.BlockSpec((None, tm, e), lambda bi, m: (bi, m, 0)),
                  pl.BlockSpec((e, d), lambda bi, m: (0, 0), pipeline_mode=pl.Buffered(1)),
                  row_block(),
                  mod_block(layer, 2),
                  pl.BlockSpec((1, d), lambda bi, m: (0, 0)),
                  mod_block(mod_layer, 0), mod_block(mod_layer, 1)],
        out_specs=out_specs,
        out_shape=out_shape,
        compiler_params=_params("arbitrary", "arbitrary"),
        name="outproj_final" if final else "outproj",
    )(u, w, x, mod, norm_g.reshape(1, d), mod, mod)
    return outs[0] if final else tuple(outs)


def _dot3(a, b):
    a_hi = a.astype(BF16)
    a_lo = (a - a_hi.astype(F32)).astype(BF16)
    b_hi = b.astype(BF16)
    b_lo = (b - b_hi.astype(F32)).astype(BF16)
    d = lambda x, y: jnp.dot(x, y, preferred_element_type=F32)
    return d(a_hi, b_hi) + d(a_lo, b_hi) + d(a_hi, b_lo)


def _filter_kernel(feat_ref, featr_ref, w1_ref, b1_ref, fr1_ref, w2_ref, b2_ref, fr2_ref,
                   wf0_ref, wb0_ref, wf1_ref, wb1_ref, dl_ref, o_ref, hid_ref):
    hp = lax.Precision.HIGHEST
    L = feat_ref.shape[0]

    @pl.when(pl.program_id(0) == 0)
    def _():
        for side, f_ref in enumerate((feat_ref, featr_ref)):
            z1 = jnp.dot(f_ref[...], w1_ref[...], precision=hp, preferred_element_type=F32) + b1_ref[...]
            h1 = jnp.sin(fr1_ref[...] * z1)
            z2 = jnp.dot(h1, w2_ref[...], precision=hp, preferred_element_type=F32) + b2_ref[...]
            hid_ref[side] = jnp.sin(fr2_ref[...] * z2)

    te = dl_ref.shape[-1]
    row = lax.broadcasted_iota(jnp.int32, (L, te), 0)
    scale = 1.0 / (L - 1)
    win_f = jnp.exp(-(row.astype(F32) * scale) * dl_ref[...])
    win_b = jnp.exp(-((L - row).astype(F32) * scale) * dl_ref[...])
    h_f = hid_ref[0]
    h_b = hid_ref[1]
    for n, (wf_ref, wb_ref) in enumerate(((wf0_ref, wb0_ref), (wf1_ref, wb1_ref))):
        hb = _dot3(h_b, wb_ref[...]) * win_b
        o_ref[n, 0:L, :] = jnp.where(row == 0, 0.0, hb).astype(o_ref.dtype)
        o_ref[n, L:2 * L, :] = (_dot3(h_f, wf_ref[...]) * win_f).astype(o_ref.dtype)


def _spectrum_kernel(cs_ref, h_ref, hr_ref, hn_ref, hq_ref, *, blk):
    n_blocks = h_ref.shape[0] // blk
    ts = h_ref.shape[-1]
    inv_n = 1.0 / (2 * blk)
    row = lax.broadcasted_iota(jnp.int32, (blk, ts), 0)
    sign = (1 - 2 * (row & 1)).astype(F32)
    scale_r = jnp.where(row == 0, inv_n, 2.0 * inv_n)
    prev = None
    nyq_rows = []
    for bi in range(n_blocks):
        hb = h_ref[bi * blk:(bi + 1) * blk, :]
        fr = jnp.dot(cs_ref[:, 0:blk], hb, preferred_element_type=F32)
        fn = jnp.dot(cs_ref[:, blk:2 * blk], hb, preferred_element_type=F32)
        hf = hb.astype(F32)
        fq = jnp.sum(sign * hf, axis=0, keepdims=True)
        h0 = hf[0:1]
        if prev is not None:
            pfr, pfn, pfq, ph0 = prev
            hr_ref[bi - 1] = (fr + sign * (pfr - ph0)) * scale_r
            hn_ref[bi - 1] = (fn + sign * pfn) * (2.0 * inv_n)
            nyq_rows.append((fq + pfq - ph0) * inv_n)
        prev = (fr, fn, fq, h0)
    pad_rows = hq_ref.shape[0] - len(nyq_rows)
    if pad_rows:
        nyq_rows.append(jnp.zeros((pad_rows, ts), F32))
    hq_ref[...] = jnp.concatenate(nyq_rows, axis=0)


def _dft_bases(blk):
    k = jnp.arange(blk, dtype=jnp.int32)
    m = (k[:, None] * k[None, :]) % (2 * blk)
    ang = m.astype(F32) * (math.pi / blk)
    return jnp.concatenate([jnp.cos(ang), jnp.sin(ang)], axis=1).astype(BF16)


def _conv_block(L):
    return min(L, HY_BLOCK)


def _hyena_spectrum(L, e, cs, f_w1, f_b1, f_fr1, f_w2, f_b2, f_fr2, f_w3):
    fw = f_w1.shape[1]
    blk = _conv_block(L)
    n_lags = 2 * (L // blk) - 1
    t = jnp.linspace(0.0, 1.0, L, dtype=F32)[:, None]
    w = (2.0 * math.pi / L) * jnp.arange(L, dtype=F32)[:, None]
    f = jnp.linspace(1e-4, HY_BANDS - 1, HY_BANDS, dtype=F32)[None, :]
    feats = jnp.concatenate([t, jnp.cos(f * w), -jnp.sin(f * w)], axis=-1)
    feats = jnp.pad(feats, ((0, 0), (0, LANE - HY_EMB)))
    feats_rev = jnp.roll(feats[::-1], 1, axis=0)
    w1 = jnp.pad(f_w1, ((0, LANE - HY_EMB), (0, 0)))
    deltas = jnp.abs(jnp.linspace(math.log(HY_TARGET) / HY_SLOW_PCT,
                                  math.log(HY_TARGET) / HY_FAST_PCT, e, dtype=F32)).reshape(1, e)
    te = 512
    ne = e // te
    small = lambda shape: pl.BlockSpec(shape, lambda i: (0, 0))
    w3spec = lambda q: pl.BlockSpec((fw, te), lambda i: (0, q * ne + i))
    h_ext = pl.pallas_call(
        _filter_kernel,
        grid=(ne,),
        in_specs=[small((L, LANE)), small((L, LANE)), small((LANE, fw)), small((1, fw)), small((1, fw)),
                  small((fw, fw)), small((1, fw)), small((1, fw)),
                  w3spec(0), w3spec(1), w3spec(2), w3spec(3),
                  pl.BlockSpec((1, te), lambda i: (0, i))],
        out_specs=pl.BlockSpec((HY_ORDER, 2 * L, te), lambda i: (0, 0, i)),
        out_shape=jax.ShapeDtypeStruct((HY_ORDER, 2 * L, e), BF16),
        scratch_shapes=[pltpu.VMEM((2, L, fw), F32)],
        compiler_params=_params("arbitrary"),
        name="hyena_filter",
    )(feats, feats_rev, w1, f_b1.reshape(1, fw), f_fr1.reshape(1, fw), f_w2, f_b2.reshape(1, fw),
      f_fr2.reshape(1, fw), f_w3, f_w3, f_w3, f_w3, deltas)

    ts = 256
    nyq_rows = -(-n_lags // 8) * 8
    hr, hn, hq = pl.pallas_call(
        functools.partial(_spectrum_kernel, blk=blk),
        grid=(HY_ORDER, e // ts),
        in_specs=[pl.BlockSpec((blk, 2 * blk), lambda n, i: (0, 0)),
                  pl.BlockSpec((None, 2 * L, ts), lambda n, i: (n, 0, i))],
        out_specs=[pl.BlockSpec((None, n_lags, blk, ts), lambda n, i: (n, 0, 0, i)),
                   pl.BlockSpec((None, n_lags, blk, ts), lambda n, i: (n, 0, 0, i)),
                   pl.BlockSpec((None, nyq_rows, ts), lambda n, i: (n, 0, i))],
        out_shape=[jax.ShapeDtypeStruct((HY_ORDER, n_lags, blk, e), F32),
                   jax.ShapeDtypeStruct((HY_ORDER, n_lags, blk, e), F32),
                   jax.ShapeDtypeStruct((HY_ORDER, nyq_rows, e), F32)],
        compiler_params=_params("arbitrary", "arbitrary"),
        name="hyena_spectrum",
    )(cs, h_ext)
    return hr, hn, hq


def _longconv_kernel(cs_ref, v_ref, x1_ref, x2_ref, g_ref, hr_ref, hn_ref, hq_ref, sk_ref, o_ref,
                     a_ref, b_ref, p_ref, q_ref, z_ref, *, blk):
    L, te = v_ref.shape
    nb = L // blk
    gr = 16
    sign = (1 - 2 * (lax.broadcasted_iota(jnp.int32, (blk, te), 0) & 1)).astype(F32)
    dot = lambda x, y: jnp.dot(x, y, preferred_element_type=F32)

    def conv(z_in_ref, n, finish):
        nyq_in = []
        for j in range(nb):
            rows = slice(j * blk, (j + 1) * blk)
            zb = z_in_ref[rows, :]
            a_ref[rows, :] = dot(cs_ref[:, 0:blk], zb)
            b_ref[rows, :] = dot(cs_ref[:, blk:2 * blk], zb)
            nyq_in.append(jnp.sum(sign * zb.astype(F32), axis=0, keepdims=True))

        def pair_products(r, carry):
            r0 = pl.multiple_of(r * gr, gr)
            at = lambda j: pl.ds(pl.multiple_of(j * blk + r0, gr), gr)
            for i in range(nb):
                acc_p = acc_q = None
                for j in range(nb):
                    lag = i - j + nb - 1
                    a = a_ref[at(j), :]
                    b = b_ref[at(j), :]
                    hr = hr_ref[n, lag, pl.ds(r0, gr), :]
                    hn = hn_ref[n, lag, pl.ds(r0, gr), :]
                    tp = a * hr - b * hn
                    tq = b * hr + a * hn
                    acc_p = tp if acc_p is None else acc_p + tp
                    acc_q = tq if acc_q is None else acc_q + tq
                p_ref[at(i), :] = acc_p.astype(BF16)
                q_ref[at(i), :] = acc_q.astype(BF16)
            return carry

        lax.fori_loop(0, blk // gr, pair_products, 0)
        skip = sk_ref[n]
        for i in range(nb):
            rows = slice(i * blk, (i + 1) * blk)
            nyq = None
            for j in range(nb):
                lag = i - j + nb - 1
                term = nyq_in[j] * hq_ref[n, lag:lag + 1, :]
                nyq = term if nyq is None else nyq + term
            y = dot(cs_ref[:, 0:blk], p_ref[rows, :]) + dot(cs_ref[:, blk:2 * blk], q_ref[rows, :]) + sign * nyq
            finish(rows, y + skip * z_in_ref[rows, :].astype(F32))

    def first(rows, y):
        z_ref[rows, :] = (x1_ref[rows, :].astype(F32) * y).astype(BF16)

    def second(rows, y):
        o_ref[rows, :] = (x2_ref[rows, :].astype(F32) * y * g_ref[rows, :].astype(F32)).astype(o_ref.dtype)

    conv(v_ref, 0, first)
    conv(z_ref, 1, second)


def _longconv(u3, g, cs, hr, hn, hq, skip):
    b, L, e3 = u3.shape
    e = e3 // 3
    blk = cs.shape[0]
    n_lags = hr.shape[1]
    te = 256
    ne = e // te
    seq = lambda off: pl.BlockSpec((None, L, te), lambda ei, bi: (bi, 0, off * ne + ei))
    spec = lambda: pl.BlockSpec((HY_ORDER, n_lags, blk, te), lambda ei, bi: (0, 0, 0, ei))
    return pl.pallas_call(
        functools.partial(_longconv_kernel, blk=blk),
        grid=(ne, b),
        in_specs=[pl.BlockSpec((blk, 2 * blk), lambda ei, bi: (0, 0)),
                  seq(0), seq(1), seq(2),
                  pl.BlockSpec((None, L, te), lambda ei, bi: (bi, 0, ei)),
                  spec(), spec(),
                  pl.BlockSpec((HY_ORDER, hq.shape[1], te), lambda ei, bi: (0, 0, ei)),
                  pl.BlockSpec((HY_ORDER, 1, te), lambda ei, bi: (0, 0, ei))],
        out_specs=pl.BlockSpec((None, L, te), lambda ei, bi: (bi, 0, ei)),
        out_shape=jax.ShapeDtypeStruct((b, L, e), BF16),
        scratch_shapes=[pltpu.VMEM((L, te), F32), pltpu.VMEM((L, te), F32),
                        pltpu.VMEM((L, te), BF16), pltpu.VMEM((L, te), BF16), pltpu.VMEM((L, te), BF16)],
        compiler_params=_params("arbitrary", "arbitrary"),
        name="hyena_longconv",
    )(cs, u3, u3, u3, g, hr, hn, hq, skip.reshape(HY_ORDER, 1, e))


def _cfmid_kernel(cur_ref, prev_ref, next_ref, g_ref, w_ref, b_ref, lg_ref, lb_ref, o_ref,
                  pad_ref, sh_ref, cv_ref, *, n_tiles):
    i = pl.program_id(1)
    tl, e = cur_ref.shape
    pad_ref[0:CF_HALO, :] = jnp.where(i > 0, prev_ref[...].astype(F32), 0.0)
    pad_ref[CF_HALO:CF_HALO + tl, :] = cur_ref[...].astype(F32)
    pad_ref[CF_HALO + tl:2 * CF_HALO + tl, :] = jnp.where(i < n_tiles - 1, next_ref[...].astype(F32), 0.0)
    rc = 32
    lc = sh_ref.shape[-1]
    sh_rows = sh_ref.shape[1]
    first = CF_HALO - (CF_KERNEL - 1) // 2

    def lane_body(li, carry):
        c0 = pl.multiple_of(li * lc, lc)
        lanes = pl.ds(c0, lc)
        for s in range(1, SUBLANES):
            sh_ref[s - 1] = pad_ref[pl.ds(s, sh_rows), lanes]
        bias = b_ref[:, lanes]
        for r in range(tl // rc):
            acc = jnp.broadcast_to(bias, (rc, lc))
            for k in range(CF_KERNEL):
                phase = (first + k) % SUBLANES
                base = r * rc + first + k - phase
                if phase == 0:
                    tap = pad_ref[pl.ds(base, rc), lanes]
                else:
                    tap = sh_ref[phase - 1, pl.ds(base, rc), :]
                acc = acc + tap * w_ref[pl.ds(k, 1), lanes]
            cv_ref[pl.ds(r * rc, rc), lanes] = acc
        return carry

    lax.fori_loop(0, e // lc, lane_body, 0)
    ng = min(tl, 128)

    def norm_body(ri, carry):
        rows = pl.ds(pl.multiple_of(ri * ng, ng), ng)
        cv = cv_ref[rows, :]
        mu = jnp.mean(cv, axis=-1, keepdims=True)
        xc = cv - mu
        var = jnp.mean(xc * xc, axis=-1, keepdims=True)
        y = xc * lax.rsqrt(var + EPS) * lg_ref[...] + lb_ref[...]
        o_ref[rows, :] = (_silu(y) * g_ref[rows, :].astype(F32)).astype(o_ref.dtype)
        return carry

    lax.fori_loop(0, tl // ng, norm_body, 0)


def _cfmid(u, g, dw_w, dw_b, ln_g, ln_b):
    b, L, e = u.shape
    tl = min(L, 256)
    n_tiles = L // tl
    hb = tl // CF_HALO
    n_halo = L // CF_HALO
    wpad = jnp.pad(dw_w, ((0, 32 - CF_KERNEL), (0, 0)))
    vec = lambda: pl.BlockSpec((1, e), lambda bi, i: (0, 0))
    return pl.pallas_call(
        functools.partial(_cfmid_kernel, n_tiles=n_tiles),
        grid=(b, n_tiles),
        in_specs=[pl.BlockSpec((None, tl, e), lambda bi, i: (bi, i, 0)),
                  pl.BlockSpec((None, CF_HALO, e), lambda bi, i: (bi, jnp.maximum(i * hb - 1, 0), 0)),
                  pl.BlockSpec((None, CF_HALO, e), lambda bi, i: (bi, jnp.minimum((i + 1) * hb, n_halo - 1), 0)),
                  pl.BlockSpec((None, tl, e), lambda bi, i: (bi, i, 0)),
                  pl.BlockSpec((32, e), lambda bi, i: (0, 0)),
                  vec(), vec(), vec()],
        out_specs=pl.BlockSpec((None, tl, e), lambda bi, i: (bi, i, 0)),
        out_shape=jax.ShapeDtypeStruct((b, L, e), BF16),
        scratch_shapes=[pltpu.VMEM((tl + 2 * CF_HALO, e), F32),
                        pltpu.VMEM((SUBLANES - 1, tl + 2 * CF_HALO - SUBLANES, 512), F32),
                        pltpu.VMEM((tl, e), F32)],
        compiler_params=_params("arbitrary", "arbitrary"),
        name="conformer_mid",
    )(u, u, u, g, wpad, dw_b.reshape(1, e), ln_g.reshape(1, e), ln_b.reshape(1, e))


def _retention_kernel(lg_ref, q_ref, k_ref, v_ref, g_ref, kc_ref, vc_ref, gg_ref, gb_ref, o_ref,
                      state_ref, oacc_ref, *, chunk):
    h = pl.program_id(1)
    L, dk = q_ref.shape
    dv = v_ref.shape[-1]
    n_lat = L // chunk
    n_ctx = kc_ref.shape[0] // chunk
    rowk = lax.broadcasted_iota(jnp.int32, (chunk, dk), 0).astype(F32)
    ri = lax.broadcasted_iota(jnp.int32, (chunk, chunk), 0)
    ci = lax.broadcasted_iota(jnp.int32, (chunk, chunk), 1)

    def run(direction):
        fwd = direction == 0
        lg = lg_ref[direction, h]
        if fwd:
            q_dec = jnp.exp(lg * (rowk + 1.0))
            k_dec = jnp.exp(lg * (chunk - 1.0 - rowk))
            rel = (ri - ci).astype(F32)
        else:
            q_dec = jnp.exp(lg * (chunk - rowk))
            k_dec = jnp.exp(lg * rowk)
            rel = (ci - ri).astype(F32)
        dmask = jnp.where(rel >= 0, jnp.exp(lg * jnp.maximum(rel, 0.0)), 0.0)
        c_dec = jnp.exp(jnp.full((1, dv), lg * chunk, F32))

        def state_update(st, kc, vc):
            kd = (kc.astype(F32) * k_dec).T.astype(BF16)
            return st * c_dec + jnp.dot(kd, vc, preferred_element_type=F32)

        state_ref[...] = jnp.zeros_like(state_ref)

        def ctx_body(j, carry):
            c = j if fwd else n_ctx - 1 - j
            r0 = pl.multiple_of(c * chunk, chunk)
            state_ref[...] = state_update(state_ref[...], kc_ref[pl.ds(r0, chunk), :], vc_ref[pl.ds(r0, chunk), :])
            return carry

        lax.fori_loop(0, n_ctx, ctx_body, 0)

        def lat_body(j, carry):
            c = j if fwd else n_lat - 1 - j
            r0 = pl.multiple_of(c * chunk, chunk)
            qc = q_ref[pl.ds(r0, chunk), :]
            kc = k_ref[pl.ds(r0, chunk), :]
            vc = v_ref[pl.ds(r0, chunk), :]
            st = state_ref[...]
            scores = lax.dot_general(qc, kc, (((1,), (1,)), ((), ())), preferred_element_type=F32) * dmask
            out = (jnp.dot(scores.astype(BF16), vc, preferred_element_type=F32)
                   + jnp.dot((qc.astype(F32) * q_dec).astype(BF16), st.astype(BF16), preferred_element_type=F32))
            state_ref[...] = state_update(st, kc, vc)
            if fwd:
                oacc_ref[pl.ds(r0, chunk), :] = out
            else:
                o = oacc_ref[pl.ds(r0, chunk), :] + out
                mu = jnp.mean(o, axis=-1, keepdims=True)
                oc = o - mu
                var = jnp.mean(oc * oc, axis=-1, keepdims=True)
                y = oc * lax.rsqrt(var + EPS) * gg_ref[...] + gb_ref[...]
                o_ref[pl.ds(r0, chunk), :] = (y * g_ref[pl.ds(r0, chunk), :].astype(F32)).astype(o_ref.dtype)
            return carry

        lax.fori_loop(0, n_lat, lat_body, 0)

    run(0)
    run(1)


def _retention(q, k, v, g, kc, vc, log_g, gn_g, gn_b):
    b, L, dqk = q.shape
    e = v.shape[-1]
    dk = dqk // RT_HEADS
    dv = e // RT_HEADS
    lc = kc.shape[1]
    chunk = 2 * RT_CHUNK if (L % (2 * RT_CHUNK) == 0 and lc % (2 * RT_CHUNK) == 0) else RT_CHUNK
    head = lambda rows, width: pl.BlockSpec((None, rows, width), lambda bi, hi: (bi, 0, hi))
    return pl.pallas_call(
        functools.partial(_retention_kernel, chunk=chunk),
        grid=(b, RT_HEADS),
        in_specs=[pl.BlockSpec(memory_space=pltpu.SMEM),
                  head(L, dk), head(L, dk), head(L, dv), head(L, dv), head(lc, dk), head(lc, dv),
                  pl.BlockSpec((1, dv), lambda bi, hi: (0, hi)),
                  pl.BlockSpec((1, dv), lambda bi, hi: (0, hi))],
        out_specs=head(L, dv),
        out_shape=jax.ShapeDtypeStruct((b, L, e), BF16),
        scratch_shapes=[pltpu.VMEM((dk, dv), F32), pltpu.VMEM((L, dv), F32)],
        compiler_params=_params("arbitrary", "arbitrary"),
        name="retention",
    )(log_g, q, k, v, g, kc, vc, gn_g.reshape(1, e), gn_b.reshape(1, e))


def _rope_tables(L, dk):
    quarter = dk // 4
    inv = ROPE_BASE ** (-jnp.arange(quarter, dtype=F32) / quarter)
    t = jnp.arange(L)
    ar = (t // GRID_W).astype(F32)[:, None] * inv
    ac = (t % GRID_W).astype(F32)[:, None] * inv
    cos = jnp.concatenate([jnp.cos(ar), jnp.cos(ar), jnp.cos(ac), jnp.cos(ac)], axis=-1)
    sin = jnp.concatenate([-jnp.sin(ar), jnp.sin(ar), -jnp.sin(ac), jnp.sin(ac)], axis=-1)
    return cos, sin


def _hyena_branch(h, w_in, conv_w, conv_b, spec, bases, skip):
    b, L, d = h.shape
    e = w_in.shape[1] // 4
    h2 = h.reshape(b * L, d)
    u3 = _proj(h2, w_in, 0, 3 * e, "conv3", seq_len=L, extra=(conv_w, conv_b)).reshape(b, L, 3 * e)
    g = _proj(h2, w_in, 3 * e, e, "silu", seq_len=L).reshape(b, L, e)
    return _longconv(u3, g, bases, spec[0], spec[1], spec[2], skip)


def _conformer_branch(h, w_in, dw_w, dw_b, ln_g, ln_b):
    b, L, d = h.shape
    e = w_in.shape[1] // 3
    h2 = h.reshape(b * L, d)
    u = _proj(h2, w_in, 0, e, "glu", seq_len=L, extra=(e,)).reshape(b, L, e)
    g = _proj(h2, w_in, 2 * e, e, "silu", seq_len=L).reshape(b, L, e)
    return _cfmid(u, g, dw_w, dw_b, ln_g, ln_b)


def _retention_branch(h_lat, h_ctx, w_in, decay_logit, gn_g, gn_b):
    b, L, d = h_lat.shape
    lc = h_ctx.shape[1]
    dqk = d
    e = (w_in.shape[1] - 2 * dqk) // 2
    dk = dqk // RT_HEADS
    k_scale = dk ** -0.5
    hl = h_lat.reshape(b * L, d)
    hc = h_ctx.reshape(b * lc, d)
    cos, sin = _rope_tables(L, dk)
    q = _proj(hl, w_in, 0, dqk, "rope", seq_len=L, extra=(cos, sin)).reshape(b, L, dqk)
    k = _proj(hl, w_in, dqk, dqk, "rope", seq_len=L, extra=(cos, sin), scale=k_scale).reshape(b, L, dqk)
    v = _proj(hl, w_in, 2 * dqk, e, "plain", seq_len=L).reshape(b, L, e)
    g = _proj(hl, w_in, 2 * dqk + e, e, "silu", seq_len=L).reshape(b, L, e)
    kc = _proj(hc, w_in, dqk, dqk, "scale", seq_len=lc, scale=k_scale).reshape(b, lc, dqk)
    vc = _proj(hc, w_in, 2 * dqk, e, "plain", seq_len=lc).reshape(b, lc, e)
    log_g = jax.nn.log_sigmoid(decay_logit.astype(F32))
    return _retention(q, k, v, g, kc, vc, log_g, gn_g, gn_b)


def kernel(x, c, ctx, c_ctx, ada_w, ada_b, norm_g, final_norm_g,
           hy_w_in, hy_conv_w, hy_conv_b, hy_f_w1, hy_f_b1, hy_f_fr1, hy_f_w2, hy_f_b2,
           hy_f_fr2, hy_f_w3, hy_skip, hy_w_out,
           cf_w_in, cf_dw_w, cf_dw_b, cf_ln_g, cf_ln_b, cf_w_out,
           rt_w_in, rt_decay_logit, rt_gn_g, rt_gn_b, rt_w_out):
    depth = ada_w.shape[0]
    b, L, d = x.shape
    lc = ctx.shape[1]
    mod = _modulation(c, c_ctx, ada_w, ada_b)
    lat_row = lambda bi: bi
    ctx_row = lambda bi: b
    bases = {}
    h_lat = _normmod(x, norm_g[0], mod, 0, lat_row)
    h_ctx = _normmod(ctx, norm_g[0], mod, 0, ctx_row)
    for i in range(depth):
        kind, j = i % N_MIXERS, i // N_MIXERS
        last = i == depth - 1
        need_ctx_out = (not last) and not (i == depth - 2 and (depth - 1) % N_MIXERS != 2)
        need_ctx_in = need_ctx_out or kind == 2
        if need_ctx_in and h_ctx is None:
            raise NotImplementedError("this layer reads a context stream that no earlier layer produced")
        u_ctx = None
        if kind == 0:
            e = hy_w_in.shape[2] // 4
            w_in = hy_w_in[j].astype(BF16)
            w_out = hy_w_out[j].astype(BF16)
            filt = (hy_f_w1[j], hy_f_b1[j], hy_f_fr1[j], hy_f_w2[j], hy_f_b2[j], hy_f_fr2[j], hy_f_w3[j])
            for seq in (L, lc):
                if seq not in bases:
                    bases[seq] = _dft_bases(_conv_block(seq))
            spec = _hyena_spectrum(L, e, bases[L], *filt)
            u_lat = _hyena_branch(h_lat, w_in, hy_conv_w[j], hy_conv_b[j], spec, bases[L], hy_skip[j])
            if need_ctx_out:
                spec_c = _hyena_spectrum(lc, e, bases[lc], *filt)
                u_ctx = _hyena_branch(h_ctx, w_in, hy_conv_w[j], hy_conv_b[j], spec_c, bases[lc], hy_skip[j])
        elif kind == 1:
            w_in = cf_w_in[j].astype(BF16)
            w_out = cf_w_out[j].astype(BF16)
            p = (cf_dw_w[j], cf_dw_b[j], cf_ln_g[j], cf_ln_b[j])
            u_lat = _conformer_branch(h_lat, w_in, *p)
            if need_ctx_out:
                u_ctx = _conformer_branch(h_ctx, w_in, *p)
        else:
            w_in = rt_w_in[j].astype(BF16)
            w_out = rt_w_out[j].astype(BF16)
            if need_ctx_out:
                raise NotImplementedError("context outputs of a retention layer are not consumed by this stack")
            u_lat = _retention_branch(h_lat, h_ctx, w_in, rt_decay_logit[j], rt_gn_g[j], rt_gn_b[j])
        if last:
            return _outproj(u_lat, w_out, x, mod, i, lat_row, final_norm_g, None)
        x, h_lat = _outproj(u_lat, w_out, x, mod, i, lat_row, norm_g[i + 1], i + 1)
        if need_ctx_out:
            ctx, h_ctx = _outproj(u_ctx, w_out, ctx, mod, i, ctx_row, norm_g[i + 1], i + 1)
        else:
            h_ctx = None
```

```python
import functools
import math

import jax
import jax.numpy as jnp
from jax import lax
from jax.experimental import pallas as pl
from jax.experimental.pallas import tpu as pltpu

F32 = jnp.float32
BF16 = jnp.bfloat16

EPS = 1e-6
N_MIXERS = 3
GRID_W = 64
HY_ORDER = 2
HY_EMB = 33
HY_BANDS = (HY_EMB - 1) // 2
HY_FAST_PCT = 0.3
HY_SLOW_PCT = 1.5
HY_TARGET = 1e-2
HY_BLOCK = 512
CF_KERNEL = 31
CF_HALO = 16
RT_HEADS = 8
RT_CHUNK = 128
ROPE_BASE = 10000.0

V7X_VMEM_BYTES = 64 * 1024 * 1024
VMEM_LIMIT_BYTES = V7X_VMEM_BYTES - 8 * 1024 * 1024
LANE = 128
SUBLANES = 8
MXU_ROWS = 256


def _params(*semantics):
    return pltpu.CompilerParams(dimension_semantics=semantics, vmem_limit_bytes=VMEM_LIMIT_BYTES)


def _mod_kernel(c_ref, w_ref, b_ref, o_ref):
    c = c_ref[...]
    m = (c * jax.nn.sigmoid(c)).astype(BF16)
    o_ref[...] = jnp.dot(m, w_ref[...].astype(BF16), preferred_element_type=F32) + b_ref[...]


def _modulation(c, c_ctx, ada_w, ada_b):
    depth, d, n3 = ada_w.shape
    b = c.shape[0]
    rows = -(-(b + 1) // 8) * 8
    cc = jnp.zeros((rows, d), F32).at[:b].set(c).at[b].set(c_ctx)
    tn = 1536
    out = pl.pallas_call(
        _mod_kernel,
        grid=(depth, n3 // tn),
        in_specs=[pl.BlockSpec((rows, d), lambda l, n: (0, 0)),
                  pl.BlockSpec((None, d, tn), lambda l, n: (l, 0, n)),
                  pl.BlockSpec((None, 1, tn), lambda l, n: (l, 0, n))],
        out_specs=pl.BlockSpec((None, rows, tn), lambda l, n: (l, 0, n)),
        out_shape=jax.ShapeDtypeStruct((depth, rows, n3), F32),
        compiler_params=_params("arbitrary", "arbitrary"),
        name="adaln_modulation",
    )(cc, ada_w, ada_b.reshape(depth, 1, n3))
    return out.reshape(depth, rows, 1, n3)


def _normmod_kernel(x_ref, g_ref, sh_ref, sc_ref, o_ref):
    x = x_ref[...]
    y = x * lax.rsqrt(jnp.mean(x * x, axis=-1, keepdims=True) + EPS)
    o_ref[...] = ((y * g_ref[...]) * (1.0 + sc_ref[...]) + sh_ref[...]).astype(o_ref.dtype)


def _normmod(x, norm_g, mod, layer, row_of):
    b, L, d = x.shape
    tl = min(L, 512)
    return pl.pallas_call(
        _normmod_kernel,
        grid=(b, L // tl),
        in_specs=[pl.BlockSpec((None, tl, d), lambda bi, i: (bi, i, 0)),
                  pl.BlockSpec((1, d), lambda bi, i: (0, 0)),
                  pl.BlockSpec((None, None, 1, d), lambda bi, i: (layer, row_of(bi), 0, 0)),
                  pl.BlockSpec((None, None, 1, d), lambda bi, i: (layer, row_of(bi), 0, 1))],
        out_specs=pl.BlockSpec((None, tl, d), lambda bi, i: (bi, i, 0)),
        out_shape=jax.ShapeDtypeStruct((b, L, d), BF16),
        compiler_params=_params("arbitrary", "arbitrary"),
        name="normmod",
    )(x, norm_g.reshape(1, d), mod, mod)


def _silu(x):
    return x * jax.nn.sigmoid(x)


def _proj_kernel(*refs, mode, seq_len, scale, mc):
    h_ref, w_ref = refs[0], refs[1]
    tm = h_ref.shape[0]
    tn = w_ref.shape[1]
    n_chunks = tm // mc
    dot = lambda x, y: jnp.dot(x, y, preferred_element_type=F32)

    if mode == "conv3":
        cw_ref, cb_ref, o_ref = refs[2:5]
        acc = dot(h_ref[...], w_ref[...])
        cw = cw_ref[...]
        row = lax.rem(lax.broadcasted_iota(jnp.int32, acc.shape, 0), seq_len)
        prev = jnp.where(row == 0, 0.0, pltpu.roll(acc, 1, axis=0))
        nxt = jnp.where(row == seq_len - 1, 0.0, pltpu.roll(acc, tm - 1, axis=0))
        o_ref[...] = (prev * cw[0:1] + acc * cw[1:2] + nxt * cw[2:3] + cb_ref[...]).astype(o_ref.dtype)
        return

    o_ref = refs[-1]
    for i in range(n_chunks):
        rows = slice(i * mc, (i + 1) * mc)
        acc = dot(h_ref[rows, :], w_ref[...])
        if mode == "plain":
            y = acc
        elif mode == "scale":
            y = acc * scale
        elif mode == "silu":
            y = _silu(acc)
        elif mode == "glu":
            y = acc * jax.nn.sigmoid(dot(h_ref[rows, :], refs[2][...]))
        elif mode == "rope":
            cos, sin = refs[2][rows, :], refs[3][rows, :]
            dk = cos.shape[-1]
            quarter = dk // 4
            lane = lax.broadcasted_iota(jnp.int32, (mc, dk), 1)
            first = lax.rem(lane, 2 * quarter) < quarter
            parts = []
            for gi in range(tn // dk):
                xg = acc[:, gi * dk:(gi + 1) * dk]
                swapped = jnp.where(first, pltpu.roll(xg, dk - quarter, axis=1), pltpu.roll(xg, quarter, axis=1))
                parts.append((xg * cos + swapped * sin) * scale)
            y = parts[0] if len(parts) == 1 else jnp.concatenate(parts, axis=1)
        else:
            raise ValueError(mode)
        o_ref[rows, :] = y.astype(o_ref.dtype)


def _proj(h2, w, col0, ncols, mode, *, seq_len, extra=(), scale=1.0):
    m_rows, d = h2.shape
    n_seq = m_rows // seq_len
    k = 1
    if mode != "rope":
        for cand in range(1, n_seq + 1):
            if n_seq % cand == 0 and seq_len * cand <= 2048:
                k = cand
    tm = seq_len * k
    mc = math.gcd(seq_len, MXU_ROWS)
    col_starts = [col0] + ([extra[0]] if mode == "glu" else [])
    wide = mode != "conv3" and ncols % 1024 == 0 and all(c % 1024 == 0 for c in col_starts)
    tn = 1024 if wide else 512
    c0 = col0 // tn
    in_specs = [pl.BlockSpec((tm, d), lambda m, n: (m, 0)),
                pl.BlockSpec((d, tn), lambda m, n: (0, c0 + n))]
    args = [h2, w]
    if mode == "glu":
        gate_c0 = extra[0] // tn
        in_specs.append(pl.BlockSpec((d, tn), lambda m, n: (0, gate_c0 + n)))
        args.append(w)
    elif mode == "conv3":
        cw, cb = extra
        in_specs += [pl.BlockSpec((cw.shape[0], tn), lambda m, n: (0, n)),
                     pl.BlockSpec((1, tn), lambda m, n: (0, n))]
        args += [cw, cb.reshape(1, -1)]
    elif mode == "rope":
        cos, sin = extra
        dk = cos.shape[-1]
        in_specs += [pl.BlockSpec((tm, dk), lambda m, n: (0, 0)),
                     pl.BlockSpec((tm, dk), lambda m, n: (0, 0))]
        args += [cos, sin]
    return pl.pallas_call(
        functools.partial(_proj_kernel, mode=mode, seq_len=seq_len, scale=scale, mc=mc),
        grid=(m_rows // tm, ncols // tn),
        in_specs=in_specs,
        out_specs=pl.BlockSpec((tm, tn), lambda m, n: (m, n)),
        out_shape=jax.ShapeDtypeStruct((m_rows, ncols), BF16),
        compiler_params=_params("arbitrary", "arbitrary"),
        name="proj_" + mode,
    )(*args)


def _outproj_kernel(u_ref, w_ref, x_ref, gate_ref, g_ref, sh_ref, sc_ref, *out_refs, final, mc):
    tm = u_ref.shape[0]
    for i in range(tm // mc):
        rows = slice(i * mc, (i + 1) * mc)
        acc = jnp.dot(u_ref[rows, :], w_ref[...], preferred_element_type=F32)
        xn = x_ref[rows, :] + gate_ref[...] * acc
        y = xn * lax.rsqrt(jnp.mean(xn * xn, axis=-1, keepdims=True) + EPS) * g_ref[...]
        if final:
            out_refs[0][rows, :] = y
        else:
            out_refs[0][rows, :] = xn
            out_refs[1][rows, :] = (y * (1.0 + sc_ref[...]) + sh_ref[...]).astype(out_refs[1].dtype)


def _outproj(u, w, x, mod, layer, row_of, norm_g, next_layer):
    b, L, e = u.shape
    d = x.shape[-1]
    tm = min(L, 512)
    mc = math.gcd(tm, MXU_ROWS)
    final = next_layer is None
    mod_layer = layer if final else next_layer
    row_block = lambda: pl.BlockSpec((None, tm, d), lambda bi, m: (bi, m, 0))
    mod_block = lambda lyr, part: pl.BlockSpec((None, None, 1, d), lambda bi, m: (lyr, row_of(bi), 0, part))
    out_specs = [row_block()] if final else [row_block(), row_block()]
    out_shape = [jax.ShapeDtypeStruct(x.shape, F32)] + ([] if final else [jax.ShapeDtypeStruct(x.shape, BF16)])
    outs = pl.pallas_call(
        functools.partial(_outproj_kernel, final=final, mc=mc),
        grid=(b, L // tm),
        in_specs=[pl.BlockSpec((None, tm, e), lambda bi, m: (bi, m, 0)),
                  pl.BlockSpec((e, d), lambda bi, m: (0, 0), pipeline_mode=pl.Buffered(1)),
                  row_block(),
                  mod_block(layer, 2),
                  pl.BlockSpec((1, d), lambda bi, m: (0, 0)),
                  mod_block(mod_layer, 0), mod_block(mod_layer, 1)],
        out_specs=out_specs,
        out_shape=out_shape,
        compiler_params=_params("arbitrary", "arbitrary"),
        name="outproj_final" if final else "outproj",
    )(u, w, x, mod, norm_g.reshape(1, d), mod, mod)
    return outs[0] if final else tuple(outs)


def _dot3(a, b):
    a_hi = a.astype(BF16)
    a_lo = (a - a_hi.astype(F32)).astype(BF16)
    b_hi = b.astype(BF16)
    b_lo = (b - b_hi.astype(F32)).astype(BF16)
    d = lambda x, y: jnp.dot(x, y, preferred_element_type=F32)
    return d(a_hi, b_hi) + d(a_lo, b_hi) + d(a_hi, b_lo)


def _filter_kernel(feat_ref, featr_ref, w1_ref, b1_ref, fr1_ref, w2_ref, b2_ref, fr2_ref,
                   wf0_ref, wb0_ref, wf1_ref, wb1_ref, dl_ref, o_ref, hid_ref):
    hp = lax.Precision.HIGHEST
    L = feat_ref.shape[0]

    @pl.when(pl.program_id(0) == 0)
    def _():
        for side, f_ref in enumerate((feat_ref, featr_ref)):
            z1 = jnp.dot(f_ref[...], w1_ref[...], precision=hp, preferred_element_type=F32) + b1_ref[...]
            h1 = jnp.sin(fr1_ref[...] * z1)
            z2 = jnp.dot(h1, w2_ref[...], precision=hp, preferred_element_type=F32) + b2_ref[...]
            hid_ref[side] = jnp.sin(fr2_ref[...] * z2)

    te = dl_ref.shape[-1]
    row = lax.broadcasted_iota(jnp.int32, (L, te), 0)
    scale = 1.0 / (L - 1)
    win_f = jnp.exp(-(row.astype(F32) * scale) * dl_ref[...])
    win_b = jnp.exp(-((L - row).astype(F32) * scale) * dl_ref[...])
    h_f = hid_ref[0]
    h_b = hid_ref[1]
    for n, (wf_ref, wb_ref) in enumerate(((wf0_ref, wb0_ref), (wf1_ref, wb1_ref))):
        hb = _dot3(h_b, wb_ref[...]) * win_b
        o_ref[n, 0:L, :] = jnp.where(row == 0, 0.0, hb).astype(o_ref.dtype)
        o_ref[n, L:2 * L, :] = (_dot3(h_f, wf_ref[...]) * win_f).astype(o_ref.dtype)


def _spectrum_kernel(cs_ref, h_ref, hr_ref, hn_ref, hq_ref, *, blk):
    n_blocks = h_ref.shape[0] // blk
    ts = h_ref.shape[-1]
    inv_n = 1.0 / (2 * blk)
    row = lax.broadcasted_iota(jnp.int32, (blk, ts), 0)
    sign = (1 - 2 * (row & 1)).astype(F32)
    scale_r = jnp.where(row == 0, inv_n, 2.0 * inv_n)
    prev = None
    nyq_rows = []
    for bi in range(n_blocks):
        hb = h_ref[bi * blk:(bi + 1) * blk, :]
        fr = jnp.dot(cs_ref[:, 0:blk], hb, preferred_element_type=F32)
        fn = jnp.dot(cs_ref[:, blk:2 * blk], hb, preferred_element_type=F32)
        hf = hb.astype(F32)
        fq = jnp.sum(sign * hf, axis=0, keepdims=True)
        h0 = hf[0:1]
        if prev is not None:
            pfr, pfn, pfq, ph0 = prev
            hr_ref[bi - 1] = (fr + sign * (pfr - ph0)) * scale_r
            hn_ref[bi - 1] = (fn + sign * pfn) * (2.0 * inv_n)
            nyq_rows.append((fq + pfq - ph0) * inv_n)
        prev = (fr, fn, fq, h0)
    pad_rows = hq_ref.shape[0] - len(nyq_rows)
    if pad_rows:
        nyq_rows.append(jnp.zeros((pad_rows, ts), F32))
    hq_ref[...] = jnp.concatenate(nyq_rows, axis=0)


def _dft_bases(blk):
    k = jnp.arange(blk, dtype=jnp.int32)
    m = (k[:, None] * k[None, :]) % (2 * blk)
    ang = m.astype(F32) * (math.pi / blk)
    return jnp.concatenate([jnp.cos(ang), jnp.sin(ang)], axis=1).astype(BF16)


def _conv_block(L):
    return min(L, HY_BLOCK)


def _hyena_spectrum(L, e, cs, f_w1, f_b1, f_fr1, f_w2, f_b2, f_fr2, f_w3):
    fw = f_w1.shape[1]
    blk = _conv_block(L)
    n_lags = 2 * (L // blk) - 1
    t = jnp.linspace(0.0, 1.0, L, dtype=F32)[:, None]
    w = (2.0 * math.pi / L) * jnp.arange(L, dtype=F32)[:, None]
    f = jnp.linspace(1e-4, HY_BANDS - 1, HY_BANDS, dtype=F32)[None, :]
    feats = jnp.concatenate([t, jnp.cos(f * w), -jnp.sin(f * w)], axis=-1)
    feats = jnp.pad(feats, ((0, 0), (0, LANE - HY_EMB)))
    feats_rev = jnp.roll(feats[::-1], 1, axis=0)
    w1 = jnp.pad(f_w1, ((0, LANE - HY_EMB), (0, 0)))
    deltas = jnp.abs(jnp.linspace(math.log(HY_TARGET) / HY_SLOW_PCT,
                                  math.log(HY_TARGET) / HY_FAST_PCT, e, dtype=F32)).reshape(1, e)
    te = 512
    ne = e // te
    small = lambda shape: pl.BlockSpec(shape, lambda i: (0, 0))
    w3spec = lambda q: pl.BlockSpec((fw, te), lambda i: (0, q * ne + i))
    h_ext = pl.pallas_call(
        _filter_kernel,
        grid=(ne,),
        in_specs=[small((L, LANE)), small((L, LANE)), small((LANE, fw)), small((1, fw)), small((1, fw)),
                  small((fw, fw)), small((1, fw)), small((1, fw)),
                  w3spec(0), w3spec(1), w3spec(2), w3spec(3),
                  pl.BlockSpec((1, te), lambda i: (0, i))],
        out_specs=pl.BlockSpec((HY_ORDER, 2 * L, te), lambda i: (0, 0, i)),
        out_shape=jax.ShapeDtypeStruct((HY_ORDER, 2 * L, e), BF16),
        scratch_shapes=[pltpu.VMEM((2, L, fw), F32)],
        compiler_params=_params("arbitrary"),
        name="hyena_filter",
    )(feats, feats_rev, w1, f_b1.reshape(1, fw), f_fr1.reshape(1, fw), f_w2, f_b2.reshape(1, fw),
      f_fr2.reshape(1, fw), f_w3, f_w3, f_w3, f_w3, deltas)

    ts = 256
    nyq_rows = -(-n_lags // 8) * 8
    hr, hn, hq = pl.pallas_call(
        functools.partial(_spectrum_kernel, blk=blk),
        grid=(HY_ORDER, e // ts),
        in_specs=[pl.BlockSpec((blk, 2 * blk), lambda n, i: (0, 0)),
                  pl.BlockSpec((None, 2 * L, ts), lambda n, i: (n, 0, i))],
        out_specs=[pl.BlockSpec((None, n_lags, blk, ts), lambda n, i: (n, 0, 0, i)),
                   pl.BlockSpec((None, n_lags, blk, ts), lambda n, i: (n, 0, 0, i)),
                   pl.BlockSpec((None, nyq_rows, ts), lambda n, i: (n, 0, i))],
        out_shape=[jax.ShapeDtypeStruct((HY_ORDER, n_lags, blk, e), F32),
                   jax.ShapeDtypeStruct((HY_ORDER, n_lags, blk, e), F32),
                   jax.ShapeDtypeStruct((HY_ORDER, nyq_rows, e), F32)],
        compiler_params=_params("arbitrary", "arbitrary"),
        name="hyena_spectrum",
    )(cs, h_ext)
    return hr, hn, hq


def _longconv_kernel(cs_ref, v_ref, x1_ref, x2_ref, g_ref, hr_ref, hn_ref, hq_ref, sk_ref, o_ref,
                     a_ref, b_ref, p_ref, q_ref, z_ref, *, blk):
    nt, L, te = v_ref.shape
    nb = L // blk
    gr = 16
    sign = (1 - 2 * (lax.broadcasted_iota(jnp.int32, (blk, te), 0) & 1)).astype(F32)
    dot = lambda x, y: jnp.dot(x, y, preferred_element_type=F32)

    def forward(t, z_in_ref):
        nyq_in = []
        for j in range(nb):
            rows = slice(j * blk, (j + 1) * blk)
            zb = z_in_ref[t, rows, :]
            a_ref[t, rows, :] = dot(cs_ref[:, 0:blk], zb)
            b_ref[t, rows, :] = dot(cs_ref[:, blk:2 * blk], zb)
            nyq_in.append(jnp.sum(sign * zb.astype(F32), axis=0, keepdims=True))
        return nyq_in

    def pair_products(t, n):
        for r0 in range(0, blk, gr):
            for i in range(nb):
                acc_p = acc_q = None
                for j in range(nb):
                    lag = i - j + nb - 1
                    a = a_ref[t, j * blk + r0:j * blk + r0 + gr, :]
                    b = b_ref[t, j * blk + r0:j * blk + r0 + gr, :]
                    hr = hr_ref[n, lag, r0:r0 + gr, :]
                    hn = hn_ref[n, lag, r0:r0 + gr, :]
                    tp = a * hr - b * hn
                    tq = b * hr + a * hn
                    acc_p = tp if acc_p is None else acc_p + tp
                    acc_q = tq if acc_q is None else acc_q + tq
                p_ref[t, i * blk + r0:i * blk + r0 + gr, :] = acc_p.astype(BF16)
                q_ref[t, i * blk + r0:i * blk + r0 + gr, :] = acc_q.astype(BF16)

    def inverse(t, z_in_ref, n, nyq_in, finish):
        skip = sk_ref[n]
        for i in range(nb):
            rows = slice(i * blk, (i + 1) * blk)
            nyq = None
            for j in range(nb):
                lag = i - j + nb - 1
                term = nyq_in[j] * hq_ref[n, lag:lag + 1, :]
                nyq = term if nyq is None else nyq + term
            y = (dot(cs_ref[:, 0:blk], p_ref[t, rows, :]) + dot(cs_ref[:, blk:2 * blk], q_ref[t, rows, :])
                 + sign * nyq)
            finish(t, rows, y + skip * z_in_ref[t, rows, :].astype(F32))

    def first(t, rows, y):
        z_ref[t, rows, :] = (x1_ref[t, rows, :].astype(F32) * y).astype(BF16)

    def second(t, rows, y):
        gated = x2_ref[t, rows, :].astype(F32) * y * g_ref[t, rows, :].astype(F32)
        o_ref[t, rows, :] = gated.astype(o_ref.dtype)

    for n, (z_in_ref, finish) in enumerate(((v_ref, first), (z_ref, second))):
        nyq_in = [forward(t, z_in_ref) for t in range(nt)]
        for t in range(nt):
            pair_products(t, n)
        for t in range(nt):
            inverse(t, z_in_ref, n, nyq_in[t], finish)


def _longconv(u3, g, cs, hr, hn, hq, skip):
    b, L, e3 = u3.shape
    e = e3 // 3
    blk = cs.shape[0]
    n_lags = hr.shape[1]
    te = 256
    ne = e // te
    nt = 2 if b % 2 == 0 else 1
    seq = lambda off: pl.BlockSpec((nt, L, te), lambda ei, bi: (bi, 0, off * ne + ei))
    spec = lambda: pl.BlockSpec((HY_ORDER, n_lags, blk, te), lambda ei, bi: (0, 0, 0, ei),
                                pipeline_mode=pl.Buffered(1))
    return pl.pallas_call(
        functools.partial(_longconv_kernel, blk=blk),
        grid=(ne, b // nt),
        in_specs=[pl.BlockSpec((blk, 2 * blk), lambda ei, bi: (0, 0)),
                  seq(0), seq(1), seq(2),
                  pl.BlockSpec((nt, L, te), lambda ei, bi: (bi, 0, ei)),
                  spec(), spec(),
                  pl.BlockSpec((HY_ORDER, hq.shape[1], te), lambda ei, bi: (0, 0, ei)),
                  pl.BlockSpec((HY_ORDER, 1, te), lambda ei, bi: (0, 0, ei))],
        out_specs=pl.BlockSpec((nt, L, te), lambda ei, bi: (bi, 0, ei)),
        out_shape=jax.ShapeDtypeStruct((b, L, e), BF16),
        scratch_shapes=[pltpu.VMEM((nt, L, te), F32), pltpu.VMEM((nt, L, te), F32),
                        pltpu.VMEM((nt, L, te), BF16), pltpu.VMEM((nt, L, te), BF16),
                        pltpu.VMEM((nt, L, te), BF16)],
        compiler_params=_params("arbitrary", "arbitrary"),
        name="hyena_longconv",
    )(cs, u3, u3, u3, g, hr, hn, hq, skip.reshape(HY_ORDER, 1, e))


def _cfmid_kernel(cur_ref, prev_ref, next_ref, g_ref, w_ref, b_ref, lg_ref, lb_ref, o_ref,
                  pad_ref, sh_ref, cv_ref, *, n_tiles):
    i = pl.program_id(1)
    tl, e = cur_ref.shape
    pad_ref[0:CF_HALO, :] = jnp.where(i > 0, prev_ref[...].astype(F32), 0.0)
    pad_ref[CF_HALO:CF_HALO + tl, :] = cur_ref[...].astype(F32)
    pad_ref[CF_HALO + tl:2 * CF_HALO + tl, :] = jnp.where(i < n_tiles - 1, next_ref[...].astype(F32), 0.0)
    rc = 32
    lc = sh_ref.shape[-1]
    sh_rows = sh_ref.shape[1]
    first = CF_HALO - (CF_KERNEL - 1) // 2

    def lane_body(li, carry):
        c0 = pl.multiple_of(li * lc, lc)
        lanes = pl.ds(c0, lc)
        for s in range(1, SUBLANES):
            sh_ref[s - 1] = pad_ref[pl.ds(s, sh_rows), lanes]
        bias = b_ref[:, lanes]
        for r in range(tl // rc):
            acc = jnp.broadcast_to(bias, (rc, lc))
            for k in range(CF_KERNEL):
                phase = (first + k) % SUBLANES
                base = r * rc + first + k - phase
                if phase == 0:
                    tap = pad_ref[pl.ds(base, rc), lanes]
                else:
                    tap = sh_ref[phase - 1, pl.ds(base, rc), :]
                acc = acc + tap * w_ref[pl.ds(k, 1), lanes]
            cv_ref[pl.ds(r * rc, rc), lanes] = acc
        return carry

    lax.fori_loop(0, e // lc, lane_body, 0)
    ng = min(tl, 128)

    def norm_body(ri, carry):
        rows = pl.ds(pl.multiple_of(ri * ng, ng), ng)
        cv = cv_ref[rows, :]
        mu = jnp.mean(cv, axis=-1, keepdims=True)
        xc = cv - mu
        var = jnp.mean(xc * xc, axis=-1, keepdims=True)
        y = xc * lax.rsqrt(var + EPS) * lg_ref[...] + lb_ref[...]
        o_ref[rows, :] = (_silu(y) * g_ref[rows, :].astype(F32)).astype(o_ref.dtype)
        return carry

    lax.fori_loop(0, tl // ng, norm_body, 0)


def _cfmid(u, g, dw_w, dw_b, ln_g, ln_b):
    b, L, e = u.shape
    tl = min(L, 256)
    n_tiles = L // tl
    hb = tl // CF_HALO
    n_halo = L // CF_HALO
    wpad = jnp.pad(dw_w, ((0, 32 - CF_KERNEL), (0, 0)))
    vec = lambda: pl.BlockSpec((1, e), lambda bi, i: (0, 0))
    return pl.pallas_call(
        functools.partial(_cfmid_kernel, n_tiles=n_tiles),
        grid=(b, n_tiles),
        in_specs=[pl.BlockSpec((None, tl, e), lambda bi, i: (bi, i, 0)),
                  pl.BlockSpec((None, CF_HALO, e), lambda bi, i: (bi, jnp.maximum(i * hb - 1, 0), 0)),
                  pl.BlockSpec((None, CF_HALO, e), lambda bi, i: (bi, jnp.minimum((i + 1) * hb, n_halo - 1), 0)),
                  pl.BlockSpec((None, tl, e), lambda bi, i: (bi, i, 0)),
                  pl.BlockSpec((32, e), lambda bi, i: (0, 0)),
                  vec(), vec(), vec()],
        out_specs=pl.BlockSpec((None, tl, e), lambda bi, i: (bi, i, 0)),
        out_shape=jax.ShapeDtypeStruct((b, L, e), BF16),
        scratch_shapes=[pltpu.VMEM((tl + 2 * CF_HALO, e), F32),
                        pltpu.VMEM((SUBLANES - 1, tl + 2 * CF_HALO - SUBLANES, 512), F32),
                        pltpu.VMEM((tl, e), F32)],
        compiler_params=_params("arbitrary", "arbitrary"),
        name="conformer_mid",
    )(u, u, u, g, wpad, dw_b.reshape(1, e), ln_g.reshape(1, e), ln_b.reshape(1, e))


def _retention_kernel(lg_ref, q_ref, k_ref, v_ref, g_ref, kc_ref, vc_ref, gg_ref, gb_ref, o_ref,
                      state_ref, oacc_ref, *, chunk):
    h = pl.program_id(1)
    L, dk = q_ref.shape
    dv = v_ref.shape[-1]
    n_lat = L // chunk
    n_ctx = kc_ref.shape[0] // chunk
    rowk = lax.broadcasted_iota(jnp.int32, (chunk, dk), 0).astype(F32)
    ri = lax.broadcasted_iota(jnp.int32, (chunk, chunk), 0)
    ci = lax.broadcasted_iota(jnp.int32, (chunk, chunk), 1)

    def run(direction):
        fwd = direction == 0
        lg = lg_ref[direction, h]
        if fwd:
            q_dec = jnp.exp(lg * (rowk + 1.0))
            k_dec = jnp.exp(lg * (chunk - 1.0 - rowk))
            rel = (ri - ci).astype(F32)
        else:
            q_dec = jnp.exp(lg * (chunk - rowk))
            k_dec = jnp.exp(lg * rowk)
            rel = (ci - ri).astype(F32)
        dmask = jnp.where(rel >= 0, jnp.exp(lg * jnp.maximum(rel, 0.0)), 0.0)
        c_dec = jnp.exp(jnp.full((1, dv), lg * chunk, F32))

        def state_update(st, kc, vc):
            kd = (kc.astype(F32) * k_dec).T.astype(BF16)
            return st * c_dec + jnp.dot(kd, vc, preferred_element_type=F32)

        state_ref[...] = jnp.zeros_like(state_ref)

        def ctx_body(j, carry):
            c = j if fwd else n_ctx - 1 - j
            r0 = pl.multiple_of(c * chunk, chunk)
            state_ref[...] = state_update(state_ref[...], kc_ref[pl.ds(r0, chunk), :], vc_ref[pl.ds(r0, chunk), :])
            return carry

        lax.fori_loop(0, n_ctx, ctx_body, 0)

        def lat_body(j, carry):
            c = j if fwd else n_lat - 1 - j
            r0 = pl.multiple_of(c * chunk, chunk)
            qc = q_ref[pl.ds(r0, chunk), :]
            kc = k_ref[pl.ds(r0, chunk), :]
            vc = v_ref[pl.ds(r0, chunk), :]
            st = state_ref[...]
            scores = lax.dot_general(qc, kc, (((1,), (1,)), ((), ())), preferred_element_type=F32) * dmask
            out = (jnp.dot(scores.astype(BF16), vc, preferred_element_type=F32)
                   + jnp.dot((qc.astype(F32) * q_dec).astype(BF16), st.astype(BF16), preferred_element_type=F32))
            state_ref[...] = state_update(st, kc, vc)
            if fwd:
                oacc_ref[pl.ds(r0, chunk), :] = out
            else:
                o = oacc_ref[pl.ds(r0, chunk), :] + out
                mu = jnp.mean(o, axis=-1, keepdims=True)
                oc = o - mu
                var = jnp.mean(oc * oc, axis=-1, keepdims=True)
                y = oc * lax.rsqrt(var + EPS) * gg_ref[...] + gb_ref[...]
                o_ref[pl.ds(r0, chunk), :] = (y * g_ref[pl.ds(r0, chunk), :].astype(F32)).astype(o_ref.dtype)
            return carry

        lax.fori_loop(0, n_lat, lat_body, 0)

    run(0)
    run(1)


def _retention(q, k, v, g, kc, vc, log_g, gn_g, gn_b):
    b, L, dqk = q.shape
    e = v.shape[-1]
    dk = dqk // RT_HEADS
    dv = e // RT_HEADS
    lc = kc.shape[1]
    chunk = 2 * RT_CHUNK if (L % (2 * RT_CHUNK) == 0 and lc % (2 * RT_CHUNK) == 0) else RT_CHUNK
    head = lambda rows, width: pl.BlockSpec((None, rows, width), lambda bi, hi: (bi, 0, hi))
    return pl.pallas_call(
        functools.partial(_retention_kernel, chunk=chunk),
        grid=(b, RT_HEADS),
        in_specs=[pl.BlockSpec(memory_space=pltpu.SMEM),
                  head(L, dk), head(L, dk), head(L, dv), head(L, dv), head(lc, dk), head(lc, dv),
                  pl.BlockSpec((1, dv), lambda bi, hi: (0, hi)),
                  pl.BlockSpec((1, dv), lambda bi, hi: (0, hi))],
        out_specs=head(L, dv),
        out_shape=jax.ShapeDtypeStruct((b, L, e), BF16),
        scratch_shapes=[pltpu.VMEM((dk, dv), F32), pltpu.VMEM((L, dv), F32)],
        compiler_params=_params("arbitrary", "arbitrary"),
        name="retention",
    )(log_g, q, k, v, g, kc, vc, gn_g.reshape(1, e), gn_b.reshape(1, e))


def _rope_tables(L, dk):
    quarter = dk // 4
    inv = ROPE_BASE ** (-jnp.arange(quarter, dtype=F32) / quarter)
    t = jnp.arange(L)
    ar = (t // GRID_W).astype(F32)[:, None] * inv
    ac = (t % GRID_W).astype(F32)[:, None] * inv
    cos = jnp.concatenate([jnp.cos(ar), jnp.cos(ar), jnp.cos(ac), jnp.cos(ac)], axis=-1)
    sin = jnp.concatenate([-jnp.sin(ar), jnp.sin(ar), -jnp.sin(ac), jnp.sin(ac)], axis=-1)
    return cos, sin


def _hyena_branch(h, w_in, conv_w, conv_b, spec, bases, skip):
    b, L, d = h.shape
    e = w_in.shape[1] // 4
    h2 = h.reshape(b * L, d)
    u3 = _proj(h2, w_in, 0, 3 * e, "conv3", seq_len=L, extra=(conv_w, conv_b)).reshape(b, L, 3 * e)
    g = _proj(h2, w_in, 3 * e, e, "silu", seq_len=L).reshape(b, L, e)
    return _longconv(u3, g, bases, spec[0], spec[1], spec[2], skip)


def _conformer_branch(h, w_in, dw_w, dw_b, ln_g, ln_b):
    b, L, d = h.shape
    e = w_in.shape[1] // 3
    h2 = h.reshape(b * L, d)
    u = _proj(h2, w_in, 0, e, "glu", seq_len=L, extra=(e,)).reshape(b, L, e)
    g = _proj(h2, w_in, 2 * e, e, "silu", seq_len=L).reshape(b, L, e)
    return _cfmid(u, g, dw_w, dw_b, ln_g, ln_b)


def _retention_branch(h_lat, h_ctx, w_in, decay_logit, gn_g, gn_b):
    b, L, d = h_lat.shape
    lc = h_ctx.shape[1]
    dqk = d
    e = (w_in.shape[1] - 2 * dqk) // 2
    dk = dqk // RT_HEADS
    k_scale = dk ** -0.5
    hl = h_lat.reshape(b * L, d)
    hc = h_ctx.reshape(b * lc, d)
    cos, sin = _rope_tables(L, dk)
    q = _proj(hl, w_in, 0, dqk, "rope", seq_len=L, extra=(cos, sin)).reshape(b, L, dqk)
    k = _proj(hl, w_in, dqk, dqk, "rope", seq_len=L, extra=(cos, sin), scale=k_scale).reshape(b, L, dqk)
    v = _proj(hl, w_in, 2 * dqk, e, "plain", seq_len=L).reshape(b, L, e)
    g = _proj(hl, w_in, 2 * dqk + e, e, "silu", seq_len=L).reshape(b, L, e)
    kc = _proj(hc, w_in, dqk, dqk, "scale", seq_len=lc, scale=k_scale).reshape(b, lc, dqk)
    vc = _proj(hc, w_in, 2 * dqk, e, "plain", seq_len=lc).reshape(b, lc, e)
    log_g = jax.nn.log_sigmoid(decay_logit.astype(F32))
    return _retention(q, k, v, g, kc, vc, log_g, gn_g, gn_b)


def kernel(x, c, ctx, c_ctx, ada_w, ada_b, norm_g, final_norm_g,
           hy_w_in, hy_conv_w, hy_conv_b, hy_f_w1, hy_f_b1, hy_f_fr1, hy_f_w2, hy_f_b2,
           hy_f_fr2, hy_f_w3, hy_skip, hy_w_out,
           cf_w_in, cf_dw_w, cf_dw_b, cf_ln_g, cf_ln_b, cf_w_out,
           rt_w_in, rt_decay_logit, rt_gn_g, rt_gn_b, rt_w_out):
    depth = ada_w.shape[0]
    b, L, d = x.shape
    lc = ctx.shape[1]
    mod = _modulation(c, c_ctx, ada_w, ada_b)
    lat_row = lambda bi: bi
    ctx_row = lambda bi: b
    bases = {}
    h_lat = _normmod(x, norm_g[0], mod, 0, lat_row)
    h_ctx = _normmod(ctx, norm_g[0], mod, 0, ctx_row)
    for i in range(depth):
        kind, j = i % N_MIXERS, i // N_MIXERS
        last = i == depth - 1
        need_ctx_out = (not last) and not (i == depth - 2 and (depth - 1) % N_MIXERS != 2)
        need_ctx_in = need_ctx_out or kind == 2
        if need_ctx_in and h_ctx is None:
            raise NotImplementedError("this layer reads a context stream that no earlier layer produced")
        u_ctx = None
        if kind == 0:
            e = hy_w_in.shape[2] // 4
            w_in = hy_w_in[j].astype(BF16)
            w_out = hy_w_out[j].astype(BF16)
            filt = (hy_f_w1[j], hy_f_b1[j], hy_f_fr1[j], hy_f_w2[j], hy_f_b2[j], hy_f_fr2[j], hy_f_w3[j])
            for seq in (L, lc):
                if seq not in bases:
                    bases[seq] = _dft_bases(_conv_block(seq))
            spec = _hyena_spectrum(L, e, bases[L], *filt)
            u_lat = _hyena_branch(h_lat, w_in, hy_conv_w[j], hy_conv_b[j], spec, bases[L], hy_skip[j])
            if need_ctx_out:
                spec_c = _hyena_spectrum(lc, e, bases[lc], *filt)
                u_ctx = _hyena_branch(h_ctx, w_in, hy_conv_w[j], hy_conv_b[j], spec_c, bases[lc], hy_skip[j])
        elif kind == 1:
            w_in = cf_w_in[j].astype(BF16)
            w_out = cf_w_out[j].astype(BF16)
            p = (cf_dw_w[j], cf_dw_b[j], cf_ln_g[j], cf_ln_b[j])
            u_lat = _conformer_branch(h_lat, w_in, *p)
            if need_ctx_out:
                u_ctx = _conformer_branch(h_ctx, w_in, *p)
        else:
            w_in = rt_w_in[j].astype(BF16)
            w_out = rt_w_out[j].astype(BF16)
            if need_ctx_out:
                raise NotImplementedError("context outputs of a retention layer are not consumed by this stack")
            u_lat = _retention_branch(h_lat, h_ctx, w_in, rt_decay_logit[j], rt_gn_g[j], rt_gn_b[j])
        if last:
            return _outproj(u_lat, w_out, x, mod, i, lat_row, final_norm_g, None)
        x, h_lat = _outproj(u_lat, w_out, x, mod, i, lat_row, norm_g[i + 1], i + 1)
        if need_ctx_out:
            ctx, h_ctx = _outproj(u_ctx, w_out, ctx, mod, i, ctx_row, norm_g[i + 1], i + 1)
        else:
            h_ctx = None
```

```python
import functools
import math

import jax
import jax.numpy as jnp
from jax import lax
from jax.experimental import pallas as pl
from jax.experimental.pallas import tpu as pltpu

F32 = jnp.float32
BF16 = jnp.bfloat16

EPS = 1e-6
N_MIXERS = 3
GRID_W = 64
HY_ORDER = 2
HY_EMB = 33
HY_BANDS = (HY_EMB - 1) // 2
HY_FAST_PCT = 0.3
HY_SLOW_PCT = 1.5
HY_TARGET = 1e-2
HY_BLOCK = 512
CF_KERNEL = 31
CF_HALO = 16
RT_HEADS = 8
RT_CHUNK = 128
ROPE_BASE = 10000.0

V7X_VMEM_BYTES = 64 * 1024 * 1024
VMEM_LIMIT_BYTES = V7X_VMEM_BYTES - 8 * 1024 * 1024
LANE = 128
SUBLANES = 8
MXU_ROWS = 256


def _params(*semantics):
    return pltpu.CompilerParams(dimension_semantics=semantics, vmem_limit_bytes=VMEM_LIMIT_BYTES)


def _mod_kernel(c_ref, w_ref, b_ref, o_ref):
    c = c_ref[...]
    m = (c * jax.nn.sigmoid(c)).astype(BF16)
    o_ref[...] = jnp.dot(m, w_ref[...].astype(BF16), preferred_element_type=F32) + b_ref[...]


def _modulation(c, c_ctx, ada_w, ada_b):
    depth, d, n3 = ada_w.shape
    b = c.shape[0]
    rows = -(-(b + 1) // 8) * 8
    cc = jnp.zeros((rows, d), F32).at[:b].set(c).at[b].set(c_ctx)
    tn = 1536
    out = pl.pallas_call(
        _mod_kernel,
        grid=(depth, n3 // tn),
        in_specs=[pl.BlockSpec((rows, d), lambda l, n: (0, 0)),
                  pl.BlockSpec((None, d, tn), lambda l, n: (l, 0, n)),
                  pl.BlockSpec((None, 1, tn), lambda l, n: (l, 0, n))],
        out_specs=pl.BlockSpec((None, rows, tn), lambda l, n: (l, 0, n)),
        out_shape=jax.ShapeDtypeStruct((depth, rows, n3), F32),
        compiler_params=_params("arbitrary", "arbitrary"),
        name="adaln_modulation",
    )(cc, ada_w, ada_b.reshape(depth, 1, n3))
    return out.reshape(depth, rows, 1, n3)


def _normmod_kernel(x_ref, g_ref, sh_ref, sc_ref, o_ref):
    x = x_ref[...]
    y = x * lax.rsqrt(jnp.mean(x * x, axis=-1, keepdims=True) + EPS)
    o_ref[...] = ((y * g_ref[...]) * (1.0 + sc_ref[...]) + sh_ref[...]).astype(o_ref.dtype)


def _normmod(x, norm_g, mod, layer, row_of):
    b, L, d = x.shape
    tl = min(L, 512)
    return pl.pallas_call(
        _normmod_kernel,
        grid=(b, L // tl),
        in_specs=[pl.BlockSpec((None, tl, d), lambda bi, i: (bi, i, 0)),
                  pl.BlockSpec((1, d), lambda bi, i: (0, 0)),
                  pl.BlockSpec((None, None, 1, d), lambda bi, i: (layer, row_of(bi), 0, 0)),
                  pl.BlockSpec((None, None, 1, d), lambda bi, i: (layer, row_of(bi), 0, 1))],
        out_specs=pl.BlockSpec((None, tl, d), lambda bi, i: (bi, i, 0)),
        out_shape=jax.ShapeDtypeStruct((b, L, d), BF16),
        compiler_params=_params("arbitrary", "arbitrary"),
        name="normmod",
    )(x, norm_g.reshape(1, d), mod, mod)


def _silu(x):
    return x * jax.nn.sigmoid(x)


def _proj_kernel(*refs, mode, seq_len, scale, mc):
    h_ref, w_ref = refs[0], refs[1]
    tm = h_ref.shape[0]
    tn = w_ref.shape[1]
    n_chunks = tm // mc
    dot = lambda x, y: jnp.dot(x, y, preferred_element_type=F32)

    if mode == "conv3":
        cw_ref, cb_ref, o_ref = refs[2:5]
        acc = jnp.concatenate([dot(h_ref[i * mc:(i + 1) * mc, :], w_ref[...]) for i in range(n_chunks)], axis=0)
        cw = cw_ref[...]
        row = lax.rem(lax.broadcasted_iota(jnp.int32, acc.shape, 0), seq_len)
        prev = jnp.where(row == 0, 0.0, pltpu.roll(acc, 1, axis=0))
        nxt = jnp.where(row == seq_len - 1, 0.0, pltpu.roll(acc, tm - 1, axis=0))
        o_ref[...] = (prev * cw[0:1] + acc * cw[1:2] + nxt * cw[2:3] + cb_ref[...]).astype(o_ref.dtype)
        return

    o_ref = refs[-1]
    for i in range(n_chunks):
        rows = slice(i * mc, (i + 1) * mc)
        acc = dot(h_ref[rows, :], w_ref[...])
        if mode == "plain":
            y = acc
        elif mode == "scale":
            y = acc * scale
        elif mode == "silu":
            y = _silu(acc)
        elif mode == "glu":
            y = acc * jax.nn.sigmoid(dot(h_ref[rows, :], refs[2][...]))
        elif mode == "rope":
            cos, sin = refs[2][rows, :], refs[3][rows, :]
            dk = cos.shape[-1]
            quarter = dk // 4
            lane = lax.broadcasted_iota(jnp.int32, (mc, dk), 1)
            first = lax.rem(lane, 2 * quarter) < quarter
            parts = []
            for gi in range(tn // dk):
                xg = acc[:, gi * dk:(gi + 1) * dk]
                swapped = jnp.where(first, pltpu.roll(xg, dk - quarter, axis=1), pltpu.roll(xg, quarter, axis=1))
                parts.append((xg * cos + swapped * sin) * scale)
            y = parts[0] if len(parts) == 1 else jnp.concatenate(parts, axis=1)
        else:
            raise ValueError(mode)
        o_ref[rows, :] = y.astype(o_ref.dtype)


def _proj(h2, w, col0, ncols, mode, *, seq_len, extra=(), scale=1.0):
    m_rows, d = h2.shape
    n_seq = m_rows // seq_len
    k = 1
    if mode != "rope":
        for cand in range(1, n_seq + 1):
            if n_seq % cand == 0 and seq_len * cand <= 2048:
                k = cand
    tm = seq_len * k
    mc = math.gcd(seq_len, MXU_ROWS)
    col_starts = [col0] + ([extra[0]] if mode == "glu" else [])
    wide = ncols % 1024 == 0 and all(c % 1024 == 0 for c in col_starts)
    tn = 1024 if wide else 512
    c0 = col0 // tn
    in_specs = [pl.BlockSpec((tm, d), lambda m, n: (m, 0)),
                pl.BlockSpec((d, tn), lambda m, n: (0, c0 + n))]
    args = [h2, w]
    if mode == "glu":
        gate_c0 = extra[0] // tn
        in_specs.append(pl.BlockSpec((d, tn), lambda m, n: (0, gate_c0 + n)))
        args.append(w)
    elif mode == "conv3":
        cw, cb = extra
        in_specs += [pl.BlockSpec((cw.shape[0], tn), lambda m, n: (0, n)),
                     pl.BlockSpec((1, tn), lambda m, n: (0, n))]
        args += [cw, cb.reshape(1, -1)]
    elif mode == "rope":
        cos, sin = extra
        dk = cos.shape[-1]
        in_specs += [pl.BlockSpec((tm, dk), lambda m, n: (0, 0)),
                     pl.BlockSpec((tm, dk), lambda m, n: (0, 0))]
        args += [cos, sin]
    return pl.pallas_call(
        functools.partial(_proj_kernel, mode=mode, seq_len=seq_len, scale=scale, mc=mc),
        grid=(m_rows // tm, ncols // tn),
        in_specs=in_specs,
        out_specs=pl.BlockSpec((tm, tn), lambda m, n: (m, n)),
        out_shape=jax.ShapeDtypeStruct((m_rows, ncols), BF16),
        compiler_params=_params("arbitrary", "arbitrary"),
        name="proj_" + mode,
    )(*args)


def _outproj_kernel(u_ref, w_ref, x_ref, gate_ref, g_ref, sh_ref, sc_ref, *out_refs, final, mc):
    tm = u_ref.shape[0]
    for i in range(tm // mc):
        rows = slice(i * mc, (i + 1) * mc)
        acc = jnp.dot(u_ref[rows, :], w_ref[...], preferred_element_type=F32)
        xn = x_ref[rows, :] + gate_ref[...] * acc
        y = xn * lax.rsqrt(jnp.mean(xn * xn, axis=-1, keepdims=True) + EPS) * g_ref[...]
        if final:
            out_refs[0][rows, :] = y
        else:
            out_refs[0][rows, :] = xn
            out_refs[1][rows, :] = (y * (1.0 + sc_ref[...]) + sh_ref[...]).astype(out_refs[1].dtype)


def _outproj(u, w, x, mod, layer, row_of, norm_g, next_layer):
    b, L, e = u.shape
    d = x.shape[-1]
    tm = min(L, 512)
    mc = math.gcd(tm, MXU_ROWS)
    final = next_layer is None
    mod_layer = layer if final else next_layer
    row_block = lambda: pl.BlockSpec((None, tm, d), lambda bi, m: (bi, m, 0))
    mod_block = lambda lyr, part: pl.BlockSpec((None, None, 1, d), lambda bi, m: (lyr, row_of(bi), 0, part))
    out_specs = [row_block()] if final else [row_block(), row_block()]
    out_shape = [jax.ShapeDtypeStruct(x.shape, F32)] + ([] if final else [jax.ShapeDtypeStruct(x.shape, BF16)])
    outs = pl.pallas_call(
        functools.partial(_outproj_kernel, final=final, mc=mc),
        grid=(b, L // tm),
        in_specs=[pl.BlockSpec((None, tm, e), lambda bi, m: (bi, m, 0)),
                  pl.BlockSpec((e, d), lambda bi, m: (0, 0), pipeline_mode=pl.Buffered(1)),
                  row_block(),
                  mod_block(layer, 2),
                  pl.BlockSpec((1, d), lambda bi, m: (0, 0)),
                  mod_block(mod_layer, 0), mod_block(mod_layer, 1)],
        out_specs=out_specs,
        out_shape=out_shape,
        compiler_params=_params("arbitrary", "arbitrary"),
        name="outproj_final" if final else "outproj",
    )(u, w, x, mod, norm_g.reshape(1, d), mod, mod)
    return outs[0] if final else tuple(outs)


def _dot3(a, b):
    a_hi = a.astype(BF16)
    a_lo = (a - a_hi.astype(F32)).astype(BF16)
    b_hi = b.astype(BF16)
    b_lo = (b - b_hi.astype(F32)).astype(BF16)
    d = lambda x, y: jnp.dot(x, y, preferred_element_type=F32)
    return d(a_hi, b_hi) + d(a_lo, b_hi) + d(a_hi, b_lo)


def _filter_kernel(feat_ref, featr_ref, w1_ref, b1_ref, fr1_ref, w2_ref, b2_ref, fr2_ref,
                   wf0_ref, wb0_ref, wf1_ref, wb1_ref, dl_ref, o_ref, hid_ref):
    hp = lax.Precision.HIGHEST
    L = feat_ref.shape[0]

    @pl.when(pl.program_id(0) == 0)
    def _():
        for side, f_ref in enumerate((feat_ref, featr_ref)):
            z1 = jnp.dot(f_ref[...], w1_ref[...], precision=hp, preferred_element_type=F32) + b1_ref[...]
            h1 = jnp.sin(fr1_ref[...] * z1)
            z2 = jnp.dot(h1, w2_ref[...], precision=hp, preferred_element_type=F32) + b2_ref[...]
            hid_ref[side] = jnp.sin(fr2_ref[...] * z2)

    te = dl_ref.shape[-1]
    row = lax.broadcasted_iota(jnp.int32, (L, te), 0)
    scale = 1.0 / (L - 1)
    win_f = jnp.exp(-(row.astype(F32) * scale) * dl_ref[...])
    win_b = jnp.exp(-((L - row).astype(F32) * scale) * dl_ref[...])
    h_f = hid_ref[0]
    h_b = hid_ref[1]
    for n, (wf_ref, wb_ref) in enumerate(((wf0_ref, wb0_ref), (wf1_ref, wb1_ref))):
        hb = _dot3(h_b, wb_ref[...]) * win_b
        o_ref[n, 0:L, :] = jnp.where(row == 0, 0.0, hb).astype(o_ref.dtype)
        o_ref[n, L:2 * L, :] = (_dot3(h_f, wf_ref[...]) * win_f).astype(o_ref.dtype)


def _spectrum_kernel(cs_ref, h_ref, hr_ref, hn_ref, hq_ref, *, blk):
    n_blocks = h_ref.shape[0] // blk
    ts = h_ref.shape[-1]
    inv_n = 1.0 / (2 * blk)
    row = lax.broadcasted_iota(jnp.int32, (blk, ts), 0)
    sign = (1 - 2 * (row & 1)).astype(F32)
    scale_r = jnp.where(row == 0, inv_n, 2.0 * inv_n)
    prev = None
    nyq_rows = []
    for bi in range(n_blocks):
        hb = h_ref[bi * blk:(bi + 1) * blk, :]
        fr = jnp.dot(cs_ref[:, 0:blk], hb, preferred_element_type=F32)
        fn = jnp.dot(cs_ref[:, blk:2 * blk], hb, preferred_element_type=F32)
        hf = hb.astype(F32)
        fq = jnp.sum(sign * hf, axis=0, keepdims=True)
        h0 = hf[0:1]
        if prev is not None:
            pfr, pfn, pfq, ph0 = prev
            hr_ref[bi - 1] = (fr + sign * (pfr - ph0)) * scale_r
            hn_ref[bi - 1] = (fn + sign * pfn) * (2.0 * inv_n)
            nyq_rows.append((fq + pfq - ph0) * inv_n)
        prev = (fr, fn, fq, h0)
    pad_rows = hq_ref.shape[0] - len(nyq_rows)
    if pad_rows:
        nyq_rows.append(jnp.zeros((pad_rows, ts), F32))
    hq_ref[...] = jnp.concatenate(nyq_rows, axis=0)


def _dft_bases(blk):
    k = jnp.arange(blk, dtype=jnp.int32)
    m = (k[:, None] * k[None, :]) % (2 * blk)
    ang = m.astype(F32) * (math.pi / blk)
    return jnp.concatenate([jnp.cos(ang), jnp.sin(ang)], axis=1).astype(BF16)


def _conv_block(L):
    return min(L, HY_BLOCK)


def _hyena_spectrum(L, e, cs, f_w1, f_b1, f_fr1, f_w2, f_b2, f_fr2, f_w3):
    fw = f_w1.shape[1]
    blk = _conv_block(L)
    n_lags = 2 * (L // blk) - 1
    t = jnp.linspace(0.0, 1.0, L, dtype=F32)[:, None]
    w = (2.0 * math.pi / L) * jnp.arange(L, dtype=F32)[:, None]
    f = jnp.linspace(1e-4, HY_BANDS - 1, HY_BANDS, dtype=F32)[None, :]
    feats = jnp.concatenate([t, jnp.cos(f * w), -jnp.sin(f * w)], axis=-1)
    feats = jnp.pad(feats, ((0, 0), (0, LANE - HY_EMB)))
    feats_rev = jnp.roll(feats[::-1], 1, axis=0)
    w1 = jnp.pad(f_w1, ((0, LANE - HY_EMB), (0, 0)))
    deltas = jnp.abs(jnp.linspace(math.log(HY_TARGET) / HY_SLOW_PCT,
                                  math.log(HY_TARGET) / HY_FAST_PCT, e, dtype=F32)).reshape(1, e)
    te = 512
    ne = e // te
    small = lambda shape: pl.BlockSpec(shape, lambda i: (0, 0))
    w3spec = lambda q: pl.BlockSpec((fw, te), lambda i: (0, q * ne + i))
    h_ext = pl.pallas_call(
        _filter_kernel,
        grid=(ne,),
        in_specs=[small((L, LANE)), small((L, LANE)), small((LANE, fw)), small((1, fw)), small((1, fw)),
                  small((fw, fw)), small((1, fw)), small((1, fw)),
                  w3spec(0), w3spec(1), w3spec(2), w3spec(3),
                  pl.BlockSpec((1, te), lambda i: (0, i))],
        out_specs=pl.BlockSpec((HY_ORDER, 2 * L, te), lambda i: (0, 0, i)),
        out_shape=jax.ShapeDtypeStruct((HY_ORDER, 2 * L, e), BF16),
        scratch_shapes=[pltpu.VMEM((2, L, fw), F32)],
        compiler_params=_params("arbitrary"),
        name="hyena_filter",
    )(feats, feats_rev, w1, f_b1.reshape(1, fw), f_fr1.reshape(1, fw), f_w2, f_b2.reshape(1, fw),
      f_fr2.reshape(1, fw), f_w3, f_w3, f_w3, f_w3, deltas)

    ts = 256
    nyq_rows = -(-n_lags // 8) * 8
    hr, hn, hq = pl.pallas_call(
        functools.partial(_spectrum_kernel, blk=blk),
        grid=(HY_ORDER, e // ts),
        in_specs=[pl.BlockSpec((blk, 2 * blk), lambda n, i: (0, 0)),
                  pl.BlockSpec((None, 2 * L, ts), lambda n, i: (n, 0, i))],
        out_specs=[pl.BlockSpec((None, n_lags, blk, ts), lambda n, i: (n, 0, 0, i)),
                   pl.BlockSpec((None, n_lags, blk, ts), lambda n, i: (n, 0, 0, i)),
                   pl.BlockSpec((None, nyq_rows, ts), lambda n, i: (n, 0, i))],
        out_shape=[jax.ShapeDtypeStruct((HY_ORDER, n_lags, blk, e), F32),
                   jax.ShapeDtypeStruct((HY_ORDER, n_lags, blk, e), F32),
                   jax.ShapeDtypeStruct((HY_ORDER, nyq_rows, e), F32)],
        compiler_params=_params("arbitrary", "arbitrary"),
        name="hyena_spectrum",
    )(cs, h_ext)
    return hr, hn, hq


def _longconv_kernel(cs_ref, v_ref, x1_ref, x2_ref, g_ref, hr_ref, hn_ref, hq_ref, sk_ref, o_ref,
                     a_ref, b_ref, p_ref, q_ref, z_ref, *, blk):
    nt, L, te = v_ref.shape
    nb = L // blk
    gr = 16
    sign = (1 - 2 * (lax.broadcasted_iota(jnp.int32, (blk, te), 0) & 1)).astype(F32)
    dot = lambda x, y: jnp.dot(x, y, preferred_element_type=F32)

    def forward(t, z_in_ref):
        nyq_in = []
        for j in range(nb):
            rows = slice(j * blk, (j + 1) * blk)
            zb = z_in_ref[t, rows, :]
            a_ref[t, rows, :] = dot(cs_ref[:, 0:blk], zb)
            b_ref[t, rows, :] = dot(cs_ref[:, blk:2 * blk], zb)
            nyq_in.append(jnp.sum(sign * zb.astype(F32), axis=0, keepdims=True))
        return nyq_in

    def pair_products(t, n):
        for r0 in range(0, blk, gr):
            for i in range(nb):
                acc_p = acc_q = None
                for j in range(nb):
                    lag = i - j + nb - 1
                    a = a_ref[t, j * blk + r0:j * blk + r0 + gr, :]
                    b = b_ref[t, j * blk + r0:j * blk + r0 + gr, :]
                    hr = hr_ref[n, lag, r0:r0 + gr, :]
                    hn = hn_ref[n, lag, r0:r0 + gr, :]
                    tp = a * hr - b * hn
                    tq = b * hr + a * hn
                    acc_p = tp if acc_p is None else acc_p + tp
                    acc_q = tq if acc_q is None else acc_q + tq
                p_ref[t, i * blk + r0:i * blk + r0 + gr, :] = acc_p.astype(BF16)
                q_ref[t, i * blk + r0:i * blk + r0 + gr, :] = acc_q.astype(BF16)

    def inverse(t, z_in_ref, n, nyq_in, finish):
        skip = sk_ref[n]
        for i in range(nb):
            rows = slice(i * blk, (i + 1) * blk)
            nyq = None
            for j in range(nb):
                lag = i - j + nb - 1
                term = nyq_in[j] * hq_ref[n, lag:lag + 1, :]
                nyq = term if nyq is None else nyq + term
            y = (dot(cs_ref[:, 0:blk], p_ref[t, rows, :]) + dot(cs_ref[:, blk:2 * blk], q_ref[t, rows, :])
                 + sign * nyq)
            finish(t, rows, y + skip * z_in_ref[t, rows, :].astype(F32))

    def first(t, rows, y):
        z_ref[t, rows, :] = (x1_ref[t, rows, :].astype(F32) * y).astype(BF16)

    def second(t, rows, y):
        gated = x2_ref[t, rows, :].astype(F32) * y * g_ref[t, rows, :].astype(F32)
        o_ref[t, rows, :] = gated.astype(o_ref.dtype)

    for n, (z_in_ref, finish) in enumerate(((v_ref, first), (z_ref, second))):
        nyq_in = [forward(t, z_in_ref) for t in range(nt)]
        for t in range(nt):
            pair_products(t, n)
        for t in range(nt):
            inverse(t, z_in_ref, n, nyq_in[t], finish)


def _longconv(u3, g, cs, hr, hn, hq, skip):
    b, L, e3 = u3.shape
    e = e3 // 3
    blk = cs.shape[0]
    n_lags = hr.shape[1]
    te = 256
    ne = e // te
    nt = 2 if b % 2 == 0 else 1
    seq = lambda off: pl.BlockSpec((nt, L, te), lambda ei, bi: (bi, 0, off * ne + ei))
    spec = lambda: pl.BlockSpec((HY_ORDER, n_lags, blk, te), lambda ei, bi: (0, 0, 0, ei),
                                pipeline_mode=pl.Buffered(1))
    return pl.pallas_call(
        functools.partial(_longconv_kernel, blk=blk),
        grid=(ne, b // nt),
        in_specs=[pl.BlockSpec((blk, 2 * blk), lambda ei, bi: (0, 0)),
                  seq(0), seq(1), seq(2),
                  pl.BlockSpec((nt, L, te), lambda ei, bi: (bi, 0, ei)),
                  spec(), spec(),
                  pl.BlockSpec((HY_ORDER, hq.shape[1], te), lambda ei, bi: (0, 0, ei)),
                  pl.BlockSpec((HY_ORDER, 1, te), lambda ei, bi: (0, 0, ei))],
        out_specs=pl.BlockSpec((nt, L, te), lambda ei, bi: (bi, 0, ei)),
        out_shape=jax.ShapeDtypeStruct((b, L, e), BF16),
        scratch_shapes=[pltpu.VMEM((nt, L, te), F32), pltpu.VMEM((nt, L, te), F32),
                        pltpu.VMEM((nt, L, te), BF16), pltpu.VMEM((nt, L, te), BF16),
                        pltpu.VMEM((nt, L, te), BF16)],
        compiler_params=_params("arbitrary", "arbitrary"),
        name="hyena_longconv",
    )(cs, u3, u3, u3, g, hr, hn, hq, skip.reshape(HY_ORDER, 1, e))


def _cfmid_kernel(cur_ref, prev_ref, next_ref, g_ref, w_ref, b_ref, lg_ref, lb_ref, o_ref,
                  pad_ref, sh_ref, cv_ref, *, n_tiles):
    i = pl.program_id(1)
    tl, e = cur_ref.shape
    pad_ref[0:CF_HALO, :] = jnp.where(i > 0, prev_ref[...].astype(F32), 0.0)
    pad_ref[CF_HALO:CF_HALO + tl, :] = cur_ref[...].astype(F32)
    pad_ref[CF_HALO + tl:2 * CF_HALO + tl, :] = jnp.where(i < n_tiles - 1, next_ref[...].astype(F32), 0.0)
    rc = 64
    lc = sh_ref.shape[-1]
    sh_rows = sh_ref.shape[1]
    first = CF_HALO - (CF_KERNEL - 1) // 2

    def lane_body(li, carry):
        c0 = pl.multiple_of(li * lc, lc)
        lanes = pl.ds(c0, lc)
        for s in range(1, SUBLANES):
            sh_ref[s - 1] = pad_ref[pl.ds(s, sh_rows), lanes]
        bias = b_ref[:, lanes]
        for r in range(tl // rc):
            acc = jnp.broadcast_to(bias, (rc, lc))
            for k in range(CF_KERNEL):
                phase = (first + k) % SUBLANES
                base = r * rc + first + k - phase
                if phase == 0:
                    tap = pad_ref[pl.ds(base, rc), lanes]
                else:
                    tap = sh_ref[phase - 1, pl.ds(base, rc), :]
                acc = acc + tap * w_ref[pl.ds(k, 1), lanes]
            cv_ref[pl.ds(r * rc, rc), lanes] = acc
        return carry

    lax.fori_loop(0, e // lc, lane_body, 0)
    ng = min(tl, 128)

    def norm_body(ri, carry):
        rows = pl.ds(pl.multiple_of(ri * ng, ng), ng)
        cv = cv_ref[rows, :]
        mu = jnp.mean(cv, axis=-1, keepdims=True)
        xc = cv - mu
        var = jnp.mean(xc * xc, axis=-1, keepdims=True)
        y = xc * lax.rsqrt(var + EPS) * lg_ref[...] + lb_ref[...]
        o_ref[rows, :] = (_silu(y) * g_ref[rows, :].astype(F32)).astype(o_ref.dtype)
        return carry

    lax.fori_loop(0, tl // ng, norm_body, 0)


def _cfmid(u, g, dw_w, dw_b, ln_g, ln_b):
    b, L, e = u.shape
    tl = min(L, 256)
    n_tiles = L // tl
    hb = tl // CF_HALO
    n_halo = L // CF_HALO
    wpad = jnp.pad(dw_w, ((0, 32 - CF_KERNEL), (0, 0)))
    vec = lambda: pl.BlockSpec((1, e), lambda bi, i: (0, 0))
    return pl.pallas_call(
        functools.partial(_cfmid_kernel, n_tiles=n_tiles),
        grid=(b, n_tiles),
        in_specs=[pl.BlockSpec((None, tl, e), lambda bi, i: (bi, i, 0)),
                  pl.BlockSpec((None, CF_HALO, e), lambda bi, i: (bi, jnp.maximum(i * hb - 1, 0), 0)),
                  pl.BlockSpec((None, CF_HALO, e), lambda bi, i: (bi, jnp.minimum((i + 1) * hb, n_halo - 1), 0)),
                  pl.BlockSpec((None, tl, e), lambda bi, i: (bi, i, 0)),
                  pl.BlockSpec((32, e), lambda bi, i: (0, 0)),
                  vec(), vec(), vec()],
        out_specs=pl.BlockSpec((None, tl, e), lambda bi, i: (bi, i, 0)),
        out_shape=jax.ShapeDtypeStruct((b, L, e), BF16),
        scratch_shapes=[pltpu.VMEM((tl + 2 * CF_HALO, e), F32),
                        pltpu.VMEM((SUBLANES - 1, tl + 2 * CF_HALO - SUBLANES, 512), F32),
                        pltpu.VMEM((tl, e), F32)],
        compiler_params=_params("arbitrary", "arbitrary"),
        name="conformer_mid",
    )(u, u, u, g, wpad, dw_b.reshape(1, e), ln_g.reshape(1, e), ln_b.reshape(1, e))


def _retention_kernel(lg_ref, q_ref, k_ref, v_ref, g_ref, kc_ref, vc_ref, gg_ref, gb_ref, o_ref,
                      state_ref, oacc_ref, *, chunk):
    h = pl.program_id(1)
    L, dk = q_ref.shape
    dv = v_ref.shape[-1]
    n_lat = L // chunk
    n_ctx = kc_ref.shape[0] // chunk
    rowk = lax.broadcasted_iota(jnp.int32, (chunk, dk), 0).astype(F32)
    ri = lax.broadcasted_iota(jnp.int32, (chunk, chunk), 0)
    ci = lax.broadcasted_iota(jnp.int32, (chunk, chunk), 1)

    def run(direction):
        fwd = direction == 0
        lg = lg_ref[direction, h]
        if fwd:
            q_dec = jnp.exp(lg * (rowk + 1.0))
            k_dec = jnp.exp(lg * (chunk - 1.0 - rowk))
            rel = (ri - ci).astype(F32)
        else:
            q_dec = jnp.exp(lg * (chunk - rowk))
            k_dec = jnp.exp(lg * rowk)
            rel = (ci - ri).astype(F32)
        dmask = jnp.where(rel >= 0, jnp.exp(lg * jnp.maximum(rel, 0.0)), 0.0)
        c_dec = jnp.exp(jnp.full((1, dv), lg * chunk, F32))

        def state_update(st, kc, vc):
            kd = (kc.astype(F32) * k_dec).T.astype(BF16)
            return st * c_dec + jnp.dot(kd, vc, preferred_element_type=F32)

        state_ref[...] = jnp.zeros_like(state_ref)

        def ctx_body(j, carry):
            c = j if fwd else n_ctx - 1 - j
            r0 = pl.multiple_of(c * chunk, chunk)
            state_ref[...] = state_update(state_ref[...], kc_ref[pl.ds(r0, chunk), :], vc_ref[pl.ds(r0, chunk), :])
            return carry

        lax.fori_loop(0, n_ctx, ctx_body, 0)

        def lat_body(j, carry):
            c = j if fwd else n_lat - 1 - j
            r0 = pl.multiple_of(c * chunk, chunk)
            qc = q_ref[pl.ds(r0, chunk), :]
            kc = k_ref[pl.ds(r0, chunk), :]
            vc = v_ref[pl.ds(r0, chunk), :]
            st = state_ref[...]
            scores = lax.dot_general(qc, kc, (((1,), (1,)), ((), ())), preferred_element_type=F32) * dmask
            out = (jnp.dot(scores.astype(BF16), vc, preferred_element_type=F32)
                   + jnp.dot((qc.astype(F32) * q_dec).astype(BF16), st.astype(BF16), preferred_element_type=F32))
            state_ref[...] = state_update(st, kc, vc)
            if fwd:
                oacc_ref[pl.ds(r0, chunk), :] = out
            else:
                o = oacc_ref[pl.ds(r0, chunk), :] + out
                mu = jnp.mean(o, axis=-1, keepdims=True)
                oc = o - mu
                var = jnp.mean(oc * oc, axis=-1, keepdims=True)
                y = oc * lax.rsqrt(var + EPS) * gg_ref[...] + gb_ref[...]
                o_ref[pl.ds(r0, chunk), :] = (y * g_ref[pl.ds(r0, chunk), :].astype(F32)).astype(o_ref.dtype)
            return carry

        lax.fori_loop(0, n_lat, lat_body, 0)

    run(0)
    run(1)


def _retention(q, k, v, g, kc, vc, log_g, gn_g, gn_b):
    b, L, dqk = q.shape
    e = v.shape[-1]
    dk = dqk // RT_HEADS
    dv = e // RT_HEADS
    lc = kc.shape[1]
    chunk = 2 * RT_CHUNK if (L % (2 * RT_CHUNK) == 0 and lc % (2 * RT_CHUNK) == 0) else RT_CHUNK
    head = lambda rows, width: pl.BlockSpec((None, rows, width), lambda bi, hi: (bi, 0, hi))
    return pl.pallas_call(
        functools.partial(_retention_kernel, chunk=chunk),
        grid=(b, RT_HEADS),
        in_specs=[pl.BlockSpec(memory_space=pltpu.SMEM),
                  head(L, dk), head(L, dk), head(L, dv), head(L, dv), head(lc, dk), head(lc, dv),
                  pl.BlockSpec((1, dv), lambda bi, hi: (0, hi)),
                  pl.BlockSpec((1, dv), lambda bi, hi: (0, hi))],
        out_specs=head(L, dv),
        out_shape=jax.ShapeDtypeStruct((b, L, e), BF16),
        scratch_shapes=[pltpu.VMEM((dk, dv), F32), pltpu.VMEM((L, dv), F32)],
        compiler_params=_params("arbitrary", "arbitrary"),
        name="retention",
    )(log_g, q, k, v, g, kc, vc, gn_g.reshape(1, e), gn_b.reshape(1, e))


def _rope_tables(L, dk):
    quarter = dk // 4
    inv = ROPE_BASE ** (-jnp.arange(quarter, dtype=F32) / quarter)
    t = jnp.arange(L)
    ar = (t // GRID_W).astype(F32)[:, None] * inv
    ac = (t % GRID_W).astype(F32)[:, None] * inv
    cos = jnp.concatenate([jnp.cos(ar), jnp.cos(ar), jnp.cos(ac), jnp.cos(ac)], axis=-1)
    sin = jnp.concatenate([-jnp.sin(ar), jnp.sin(ar), -jnp.sin(ac), jnp.sin(ac)], axis=-1)
    return cos, sin


def _hyena_branch(h, w_in, conv_w, conv_b, spec, bases, skip):
    b, L, d = h.shape
    e = w_in.shape[1] // 4
    h2 = h.reshape(b * L, d)
    u3 = _proj(h2, w_in, 0, 3 * e, "conv3", seq_len=L, extra=(conv_w, conv_b)).reshape(b, L, 3 * e)
    g = _proj(h2, w_in, 3 * e, e, "silu", seq_len=L).reshape(b, L, e)
    return _longconv(u3, g, bases, spec[0], spec[1], spec[2], skip)


def _conformer_branch(h, w_in, dw_w, dw_b, ln_g, ln_b):
    b, L, d = h.shape
    e = w_in.shape[1] // 3
    h2 = h.reshape(b * L, d)
    u = _proj(h2, w_in, 0, e, "glu", seq_len=L, extra=(e,)).reshape(b, L, e)
    g = _proj(h2, w_in, 2 * e, e, "silu", seq_len=L).reshape(b, L, e)
    return _cfmid(u, g, dw_w, dw_b, ln_g, ln_b)


def _retention_branch(h_lat, h_ctx, w_in, decay_logit, gn_g, gn_b):
    b, L, d = h_lat.shape
    lc = h_ctx.shape[1]
    dqk = d
    e = (w_in.shape[1] - 2 * dqk) // 2
    dk = dqk // RT_HEADS
    k_scale = dk ** -0.5
    hl = h_lat.reshape(b * L, d)
    hc = h_ctx.reshape(b * lc, d)
    cos, sin = _rope_tables(L, dk)
    q = _proj(hl, w_in, 0, dqk, "rope", seq_len=L, extra=(cos, sin)).reshape(b, L, dqk)
    k = _proj(hl, w_in, dqk, dqk, "rope", seq_len=L, extra=(cos, sin), scale=k_scale).reshape(b, L, dqk)
    v = _proj(hl, w_in, 2 * dqk, e, "plain", seq_len=L).reshape(b, L, e)
    g = _proj(hl, w_in, 2 * dqk + e, e, "silu", seq_len=L).reshape(b, L, e)
    kc = _proj(hc, w_in, dqk, dqk, "scale", seq_len=lc, scale=k_scale).reshape(b, lc, dqk)
    vc = _proj(hc, w_in, 2 * dqk, e, "plain", seq_len=lc).reshape(b, lc, e)
    log_g = jax.nn.log_sigmoid(decay_logit.astype(F32))
    return _retention(q, k, v, g, kc, vc, log_g, gn_g, gn_b)


def kernel(x, c, ctx, c_ctx, ada_w, ada_b, norm_g, final_norm_g,
           hy_w_in, hy_conv_w, hy_conv_b, hy_f_w1, hy_f_b1, hy_f_fr1, hy_f_w2, hy_f_b2,
           hy_f_fr2, hy_f_w3, hy_skip, hy_w_out,
           cf_w_in, cf_dw_w, cf_dw_b, cf_ln_g, cf_ln_b, cf_w_out,
           rt_w_in, rt_decay_logit, rt_gn_g, rt_gn_b, rt_w_out):
    depth = ada_w.shape[0]
    b, L, d = x.shape
    lc = ctx.shape[1]
    mod = _modulation(c, c_ctx, ada_w, ada_b)
    lat_row = lambda bi: bi
    ctx_row = lambda bi: b
    bases = {}
    h_lat = _normmod(x, norm_g[0], mod, 0, lat_row)
    h_ctx = _normmod(ctx, norm_g[0], mod, 0, ctx_row)
    for i in range(depth):
        kind, j = i % N_MIXERS, i // N_MIXERS
        last = i == depth - 1
        need_ctx_out = (not last) and not (i == depth - 2 and (depth - 1) % N_MIXERS != 2)
        need_ctx_in = need_ctx_out or kind == 2
        if need_ctx_in and h_ctx is None:
            raise NotImplementedError("this layer reads a context stream that no earlier layer produced")
        u_ctx = None
        if kind == 0:
            e = hy_w_in.shape[2] // 4
            w_in = hy_w_in[j].astype(BF16)
            w_out = hy_w_out[j].astype(BF16)
            filt = (hy_f_w1[j], hy_f_b1[j], hy_f_fr1[j], hy_f_w2[j], hy_f_b2[j], hy_f_fr2[j], hy_f_w3[j])
            for seq in (L, lc):
                if seq not in bases:
                    bases[seq] = _dft_bases(_conv_block(seq))
            spec = _hyena_spectrum(L, e, bases[L], *filt)
            u_lat = _hyena_branch(h_lat, w_in, hy_conv_w[j], hy_conv_b[j], spec, bases[L], hy_skip[j])
            if need_ctx_out:
                spec_c = _hyena_spectrum(lc, e, bases[lc], *filt)
                u_ctx = _hyena_branch(h_ctx, w_in, hy_conv_w[j], hy_conv_b[j], spec_c, bases[lc], hy_skip[j])
        elif kind == 1:
            w_in = cf_w_in[j].astype(BF16)
            w_out = cf_w_out[j].astype(BF16)
            p = (cf_dw_w[j], cf_dw_b[j], cf_ln_g[j], cf_ln_b[j])
            u_lat = _conformer_branch(h_lat, w_in, *p)
            if need_ctx_out:
                u_ctx = _conformer_branch(h_ctx, w_in, *p)
        else:
            w_in = rt_w_in[j].astype(BF16)
            w_out = rt_w_out[j].astype(BF16)
            if need_ctx_out:
                raise NotImplementedError("context outputs of a retention layer are not consumed by this stack")
            u_lat = _retention_branch(h_lat, h_ctx, w_in, rt_decay_logit[j], rt_gn_g[j], rt_gn_b[j])
        if last:
            return _outproj(u_lat, w_out, x, mod, i, lat_row, final_norm_g, None)
        x, h_lat = _outproj(u_lat, w_out, x, mod, i, lat_row, norm_g[i + 1], i + 1)
        if need_ctx_out:
            ctx, h_ctx = _outproj(u_ctx, w_out, ctx, mod, i, ctx_row, norm_g[i + 1], i + 1)
        else:
            h_ctx = None
```

```python
import functools
import math

import jax
import jax.numpy as jnp
from jax import lax
from jax.experimental import pallas as pl
from jax.experimental.pallas import tpu as pltpu

F32 = jnp.float32
BF16 = jnp.bfloat16

EPS = 1e-6
N_MIXERS = 3
GRID_W = 64
HY_ORDER = 2
HY_EMB = 33
HY_BANDS = (HY_EMB - 1) // 2
HY_FAST_PCT = 0.3
HY_SLOW_PCT = 1.5
HY_TARGET = 1e-2
HY_BLOCK = 512
CF_KERNEL = 31
CF_HALO = 16
RT_HEADS = 8
RT_CHUNK = 128
ROPE_BASE = 10000.0

V7X_VMEM_BYTES = 64 * 1024 * 1024
VMEM_LIMIT_BYTES = V7X_VMEM_BYTES - 8 * 1024 * 1024
LANE = 128
SUBLANES = 8
MXU_ROWS = 256


def _params(*semantics):
    return pltpu.CompilerParams(dimension_semantics=semantics, vmem_limit_bytes=VMEM_LIMIT_BYTES)


def _mod_kernel(c_ref, w_ref, b_ref, o_ref):
    c = c_ref[...]
    m = (c * jax.nn.sigmoid(c)).astype(BF16)
    o_ref[...] = jnp.dot(m, w_ref[...].astype(BF16), preferred_element_type=F32) + b_ref[...]


def _modulation(c, c_ctx, ada_w, ada_b):
    depth, d, n3 = ada_w.shape
    b = c.shape[0]
    rows = -(-(b + 1) // 8) * 8
    cc = jnp.zeros((rows, d), F32).at[:b].set(c).at[b].set(c_ctx)
    tn = 1536
    out = pl.pallas_call(
        _mod_kernel,
        grid=(depth, n3 // tn),
        in_specs=[pl.BlockSpec((rows, d), lambda l, n: (0, 0)),
                  pl.BlockSpec((None, d, tn), lambda l, n: (l, 0, n)),
                  pl.BlockSpec((None, 1, tn), lambda l, n: (l, 0, n))],
        out_specs=pl.BlockSpec((None, rows, tn), lambda l, n: (l, 0, n)),
        out_shape=jax.ShapeDtypeStruct((depth, rows, n3), F32),
        compiler_params=_params("arbitrary", "arbitrary"),
        name="adaln_modulation",
    )(cc, ada_w, ada_b.reshape(depth, 1, n3))
    return out.reshape(depth, rows, 1, n3)


def _normmod_kernel(x_ref, g_ref, sh_ref, sc_ref, o_ref):
    x = x_ref[...]
    y = x * lax.rsqrt(jnp.mean(x * x, axis=-1, keepdims=True) + EPS)
    o_ref[...] = ((y * g_ref[...]) * (1.0 + sc_ref[...]) + sh_ref[...]).astype(o_ref.dtype)


def _normmod(x, norm_g, mod, layer, row_of):
    b, L, d = x.shape
    tl = min(L, 512)
    return pl.pallas_call(
        _normmod_kernel,
        grid=(b, L // tl),
        in_specs=[pl.BlockSpec((None, tl, d), lambda bi, i: (bi, i, 0)),
                  pl.BlockSpec((1, d), lambda bi, i: (0, 0)),
                  pl.BlockSpec((None, None, 1, d), lambda bi, i: (layer, row_of(bi), 0, 0)),
                  pl.BlockSpec((None, None, 1, d), lambda bi, i: (layer, row_of(bi), 0, 1))],
        out_specs=pl.BlockSpec((None, tl, d), lambda bi, i: (bi, i, 0)),
        out_shape=jax.ShapeDtypeStruct((b, L, d), BF16),
        compiler_params=_params("arbitrary", "arbitrary"),
        name="normmod",
    )(x, norm_g.reshape(1, d), mod, mod)


def _silu(x):
    return x * jax.nn.sigmoid(x)


def _proj_kernel(*refs, mode, seq_len, scale, mc):
    h_ref, w_ref = refs[0], refs[1]
    tm = h_ref.shape[0]
    tn = w_ref.shape[1]
    n_chunks = tm // mc
    dot = lambda x, y: jnp.dot(x, y, preferred_element_type=F32)

    if mode == "conv3":
        cw_ref, cb_ref, o_ref = refs[2:5]
        acc = jnp.concatenate([dot(h_ref[i * mc:(i + 1) * mc, :], w_ref[...]) for i in range(n_chunks)], axis=0)
        cw = cw_ref[...]
        row = lax.rem(lax.broadcasted_iota(jnp.int32, acc.shape, 0), seq_len)
        prev = jnp.where(row == 0, 0.0, pltpu.roll(acc, 1, axis=0))
        nxt = jnp.where(row == seq_len - 1, 0.0, pltpu.roll(acc, tm - 1, axis=0))
        o_ref[...] = (prev * cw[0:1] + acc * cw[1:2] + nxt * cw[2:3] + cb_ref[...]).astype(o_ref.dtype)
        return

    o_ref = refs[-1]
    for i in range(n_chunks):
        rows = slice(i * mc, (i + 1) * mc)
        acc = dot(h_ref[rows, :], w_ref[...])
        if mode == "plain":
            y = acc
        elif mode == "scale":
            y = acc * scale
        elif mode == "silu":
            y = _silu(acc)
        elif mode == "glu":
            y = acc * jax.nn.sigmoid(dot(h_ref[rows, :], refs[2][...]))
        elif mode == "rope":
            cos, sin = refs[2][rows, :], refs[3][rows, :]
            dk = cos.shape[-1]
            quarter = dk // 4
            lane = lax.broadcasted_iota(jnp.int32, (mc, dk), 1)
            first = lax.rem(lane, 2 * quarter) < quarter
            parts = []
            for gi in range(tn // dk):
                xg = acc[:, gi * dk:(gi + 1) * dk]
                swapped = jnp.where(first, pltpu.roll(xg, dk - quarter, axis=1), pltpu.roll(xg, quarter, axis=1))
                parts.append((xg * cos + swapped * sin) * scale)
            y = parts[0] if len(parts) == 1 else jnp.concatenate(parts, axis=1)
        else:
            raise ValueError(mode)
        o_ref[rows, :] = y.astype(o_ref.dtype)


def _proj(h2, w, col0, ncols, mode, *, seq_len, extra=(), scale=1.0):
    m_rows, d = h2.shape
    n_seq = m_rows // seq_len
    k = 1
    if mode != "rope":
        for cand in range(1, n_seq + 1):
            if n_seq % cand == 0 and seq_len * cand <= 2048:
                k = cand
    tm = seq_len * k
    mc = math.gcd(seq_len, MXU_ROWS)
    col_starts = [col0] + ([extra[0]] if mode == "glu" else [])
    wide = ncols % 1024 == 0 and all(c % 1024 == 0 for c in col_starts)
    tn = 1024 if wide else 512
    c0 = col0 // tn
    in_specs = [pl.BlockSpec((tm, d), lambda m, n: (m, 0)),
                pl.BlockSpec((d, tn), lambda m, n: (0, c0 + n))]
    args = [h2, w]
    if mode == "glu":
        gate_c0 = extra[0] // tn
        in_specs.append(pl.BlockSpec((d, tn), lambda m, n: (0, gate_c0 + n)))
        args.append(w)
    elif mode == "conv3":
        cw, cb = extra
        in_specs += [pl.BlockSpec((cw.shape[0], tn), lambda m, n: (0, n)),
                     pl.BlockSpec((1, tn), lambda m, n: (0, n))]
        args += [cw, cb.reshape(1, -1)]
    elif mode == "rope":
        cos, sin = extra
        dk = cos.shape[-1]
        in_specs += [pl.BlockSpec((tm, dk), lambda m, n: (0, 0)),
                     pl.BlockSpec((tm, dk), lambda m, n: (0, 0))]
        args += [cos, sin]
    return pl.pallas_call(
        functools.partial(_proj_kernel, mode=mode, seq_len=seq_len, scale=scale, mc=mc),
        grid=(m_rows // tm, ncols // tn),
        in_specs=in_specs,
        out_specs=pl.BlockSpec((tm, tn), lambda m, n: (m, n)),
        out_shape=jax.ShapeDtypeStruct((m_rows, ncols), BF16),
        compiler_params=_params("arbitrary", "arbitrary"),
        name="proj_" + mode,
    )(*args)


def _outproj_kernel(u_ref, w_ref, x_ref, gate_ref, g_ref, sh_ref, sc_ref, *out_refs, final, mc):
    tm = u_ref.shape[0]
    for i in range(tm // mc):
        rows = slice(i * mc, (i + 1) * mc)
        acc = jnp.dot(u_ref[rows, :], w_ref[...], preferred_element_type=F32)
        xn = x_ref[rows, :] + gate_ref[...] * acc
        y = xn * lax.rsqrt(jnp.mean(xn * xn, axis=-1, keepdims=True) + EPS) * g_ref[...]
        if final:
            out_refs[0][rows, :] = y
        else:
            out_refs[0][rows, :] = xn
            out_refs[1][rows, :] = (y * (1.0 + sc_ref[...]) + sh_ref[...]).astype(out_refs[1].dtype)


def _outproj(u, w, x, mod, layer, row_of, norm_g, next_layer):
    b, L, e = u.shape
    d = x.shape[-1]
    tm = min(L, 512)
    mc = math.gcd(tm, MXU_ROWS)
    final = next_layer is None
    mod_layer = layer if final else next_layer
    row_block = lambda: pl.BlockSpec((None, tm, d), lambda bi, m: (bi, m, 0))
    mod_block = lambda lyr, part: pl.BlockSpec((None, None, 1, d), lambda bi, m: (lyr, row_of(bi), 0, part))
    out_specs = [row_block()] if final else [row_block(), row_block()]
    out_shape = [jax.ShapeDtypeStruct(x.shape, F32)] + ([] if final else [jax.ShapeDtypeStruct(x.shape, BF16)])
    outs = pl.pallas_call(
        functools.partial(_outproj_kernel, final=final, mc=mc),
        grid=(b, L // tm),
        in_specs=[pl.BlockSpec((None, tm, e), lambda bi, m: (bi, m, 0)),
                  pl.BlockSpec((e, d), lambda bi, m: (0, 0), pipeline_mode=pl.Buffered(1)),
                  row_block(),
                  mod_block(layer, 2),
                  pl.BlockSpec((1, d), lambda bi, m: (0, 0)),
                  mod_block(mod_layer, 0), mod_block(mod_layer, 1)],
        out_specs=out_specs,
        out_shape=out_shape,
        compiler_params=_params("arbitrary", "arbitrary"),
        name="outproj_final" if final else "outproj",
    )(u, w, x, mod, norm_g.reshape(1, d), mod, mod)
    return outs[0] if final else tuple(outs)


def _dot3(a, b):
    a_hi = a.astype(BF16)
    a_lo = (a - a_hi.astype(F32)).astype(BF16)
    b_hi = b.astype(BF16)
    b_lo = (b - b_hi.astype(F32)).astype(BF16)
    d = lambda x, y: jnp.dot(x, y, preferred_element_type=F32)
    return d(a_hi, b_hi) + d(a_lo, b_hi) + d(a_hi, b_lo)


def _filter_kernel(feat_ref, featr_ref, w1_ref, b1_ref, fr1_ref, w2_ref, b2_ref, fr2_ref,
                   wf0_ref, wb0_ref, wf1_ref, wb1_ref, dl_ref, o_ref, hid_ref):
    hp = lax.Precision.HIGHEST
    L = feat_ref.shape[0]

    @pl.when(pl.program_id(0) == 0)
    def _():
        for side, f_ref in enumerate((feat_ref, featr_ref)):
            z1 = jnp.dot(f_ref[...], w1_ref[...], precision=hp, preferred_element_type=F32) + b1_ref[...]
            h1 = jnp.sin(fr1_ref[...] * z1)
            z2 = jnp.dot(h1, w2_ref[...], precision=hp, preferred_element_type=F32) + b2_ref[...]
            hid_ref[side] = jnp.sin(fr2_ref[...] * z2)

    te = dl_ref.shape[-1]
    row = lax.broadcasted_iota(jnp.int32, (L, te), 0)
    scale = 1.0 / (L - 1)
    win_f = jnp.exp(-(row.astype(F32) * scale) * dl_ref[...])
    win_b = jnp.exp(-((L - row).astype(F32) * scale) * dl_ref[...])
    h_f = hid_ref[0]
    h_b = hid_ref[1]
    for n, (wf_ref, wb_ref) in enumerate(((wf0_ref, wb0_ref), (wf1_ref, wb1_ref))):
        hb = _dot3(h_b, wb_ref[...]) * win_b
        o_ref[n, 0:L, :] = jnp.where(row == 0, 0.0, hb).astype(o_ref.dtype)
        o_ref[n, L:2 * L, :] = (_dot3(h_f, wf_ref[...]) * win_f).astype(o_ref.dtype)


def _spectrum_kernel(cs_ref, h_ref, hr_ref, hn_ref, hq_ref, *, blk):
    n_blocks = h_ref.shape[0] // blk
    ts = h_ref.shape[-1]
    inv_n = 1.0 / (2 * blk)
    row = lax.broadcasted_iota(jnp.int32, (blk, ts), 0)
    sign = (1 - 2 * (row & 1)).astype(F32)
    scale_r = jnp.where(row == 0, inv_n, 2.0 * inv_n)
    prev = None
    nyq_rows = []
    for bi in range(n_blocks):
        hb = h_ref[bi * blk:(bi + 1) * blk, :]
        fr = jnp.dot(cs_ref[:, 0:blk], hb, preferred_element_type=F32)
        fn = jnp.dot(cs_ref[:, blk:2 * blk], hb, preferred_element_type=F32)
        hf = hb.astype(F32)
        fq = jnp.sum(sign * hf, axis=0, keepdims=True)
        h0 = hf[0:1]
        if prev is not None:
            pfr, pfn, pfq, ph0 = prev
            hr_ref[bi - 1] = (fr + sign * (pfr - ph0)) * scale_r
            hn_ref[bi - 1] = (fn + sign * pfn) * (2.0 * inv_n)
            nyq_rows.append((fq + pfq - ph0) * inv_n)
        prev = (fr, fn, fq, h0)
    pad_rows = hq_ref.shape[0] - len(nyq_rows)
    if pad_rows:
        nyq_rows.append(jnp.zeros((pad_rows, ts), F32))
    hq_ref[...] = jnp.concatenate(nyq_rows, axis=0)


def _dft_bases(blk):
    k = jnp.arange(blk, dtype=jnp.int32)
    m = (k[:, None] * k[None, :]) % (2 * blk)
    ang = m.astype(F32) * (math.pi / blk)
    return jnp.concatenate([jnp.cos(ang), jnp.sin(ang)], axis=1).astype(BF16)


def _conv_block(L):
    return min(L, HY_BLOCK)


def _hyena_spectrum(L, e, cs, f_w1, f_b1, f_fr1, f_w2, f_b2, f_fr2, f_w3):
    fw = f_w1.shape[1]
    blk = _conv_block(L)
    n_lags = 2 * (L // blk) - 1
    t = jnp.linspace(0.0, 1.0, L, dtype=F32)[:, None]
    w = (2.0 * math.pi / L) * jnp.arange(L, dtype=F32)[:, None]
    f = jnp.linspace(1e-4, HY_BANDS - 1, HY_BANDS, dtype=F32)[None, :]
    feats = jnp.concatenate([t, jnp.cos(f * w), -jnp.sin(f * w)], axis=-1)
    feats = jnp.pad(feats, ((0, 0), (0, LANE - HY_EMB)))
    feats_rev = jnp.roll(feats[::-1], 1, axis=0)
    w1 = jnp.pad(f_w1, ((0, LANE - HY_EMB), (0, 0)))
    deltas = jnp.abs(jnp.linspace(math.log(HY_TARGET) / HY_SLOW_PCT,
                                  math.log(HY_TARGET) / HY_FAST_PCT, e, dtype=F32)).reshape(1, e)
    te = 512
    ne = e // te
    small = lambda shape: pl.BlockSpec(shape, lambda i: (0, 0))
    w3spec = lambda q: pl.BlockSpec((fw, te), lambda i: (0, q * ne + i))
    h_ext = pl.pallas_call(
        _filter_kernel,
        grid=(ne,),
        in_specs=[small((L, LANE)), small((L, LANE)), small((LANE, fw)), small((1, fw)), small((1, fw)),
                  small((fw, fw)), small((1, fw)), small((1, fw)),
                  w3spec(0), w3spec(1), w3spec(2), w3spec(3),
                  pl.BlockSpec((1, te), lambda i: (0, i))],
        out_specs=pl.BlockSpec((HY_ORDER, 2 * L, te), lambda i: (0, 0, i)),
        out_shape=jax.ShapeDtypeStruct((HY_ORDER, 2 * L, e), BF16),
        scratch_shapes=[pltpu.VMEM((2, L, fw), F32)],
        compiler_params=_params("arbitrary"),
        name="hyena_filter",
    )(feats, feats_rev, w1, f_b1.reshape(1, fw), f_fr1.reshape(1, fw), f_w2, f_b2.reshape(1, fw),
      f_fr2.reshape(1, fw), f_w3, f_w3, f_w3, f_w3, deltas)

    ts = 256
    nyq_rows = -(-n_lags // 8) * 8
    hr, hn, hq = pl.pallas_call(
        functools.partial(_spectrum_kernel, blk=blk),
        grid=(HY_ORDER, e // ts),
        in_specs=[pl.BlockSpec((blk, 2 * blk), lambda n, i: (0, 0)),
                  pl.BlockSpec((None, 2 * L, ts), lambda n, i: (n, 0, i))],
        out_specs=[pl.BlockSpec((None, n_lags, blk, ts), lambda n, i: (n, 0, 0, i)),
                   pl.BlockSpec((None, n_lags, blk, ts), lambda n, i: (n, 0, 0, i)),
                   pl.BlockSpec((None, nyq_rows, ts), lambda n, i: (n, 0, i))],
        out_shape=[jax.ShapeDtypeStruct((HY_ORDER, n_lags, blk, e), F32),
                   jax.ShapeDtypeStruct((HY_ORDER, n_lags, blk, e), F32),
                   jax.ShapeDtypeStruct((HY_ORDER, nyq_rows, e), F32)],
        compiler_params=_params("arbitrary", "arbitrary"),
        name="hyena_spectrum",
    )(cs, h_ext)
    return hr, hn, hq


def _longconv_kernel(cs_ref, v_ref, x1_ref, x2_ref, g_ref, hr_ref, hn_ref, hq_ref, sk_ref, o_ref,
                     a_ref, b_ref, p_ref, q_ref, z_ref, *, blk):
    nt, L, te = v_ref.shape
    nb = L // blk
    gr = 16
    sign = (1 - 2 * (lax.broadcasted_iota(jnp.int32, (blk, te), 0) & 1)).astype(F32)
    dot = lambda x, y: jnp.dot(x, y, preferred_element_type=F32)

    def forward(t, z_in_ref):
        nyq_in = []
        for j in range(nb):
            rows = slice(j * blk, (j + 1) * blk)
            zb = z_in_ref[t, rows, :]
            a_ref[t, rows, :] = dot(cs_ref[:, 0:blk], zb)
            b_ref[t, rows, :] = dot(cs_ref[:, blk:2 * blk], zb)
            nyq_in.append(jnp.sum(sign * zb.astype(F32), axis=0, keepdims=True))
        return nyq_in

    def pair_products(t, n):
        for r0 in range(0, blk, gr):
            for i in range(nb):
                acc_p = acc_q = None
                for j in range(nb):
                    lag = i - j + nb - 1
                    a = a_ref[t, j * blk + r0:j * blk + r0 + gr, :]
                    b = b_ref[t, j * blk + r0:j * blk + r0 + gr, :]
                    hr = hr_ref[n, lag, r0:r0 + gr, :]
                    hn = hn_ref[n, lag, r0:r0 + gr, :]
                    tp = a * hr - b * hn
                    tq = b * hr + a * hn
                    acc_p = tp if acc_p is None else acc_p + tp
                    acc_q = tq if acc_q is None else acc_q + tq
                p_ref[t, i * blk + r0:i * blk + r0 + gr, :] = acc_p.astype(BF16)
                q_ref[t, i * blk + r0:i * blk + r0 + gr, :] = acc_q.astype(BF16)

    def inverse(t, z_in_ref, n, nyq_in, finish):
        skip = sk_ref[n]
        for i in range(nb):
            rows = slice(i * blk, (i + 1) * blk)
            nyq = None
            for j in range(nb):
                lag = i - j + nb - 1
                term = nyq_in[j] * hq_ref[n, lag:lag + 1, :]
                nyq = term if nyq is None else nyq + term
            y = (dot(cs_ref[:, 0:blk], p_ref[t, rows, :]) + dot(cs_ref[:, blk:2 * blk], q_ref[t, rows, :])
                 + sign * nyq)
            finish(t, rows, y + skip * z_in_ref[t, rows, :].astype(F32))

    def first(t, rows, y):
        z_ref[t, rows, :] = (x1_ref[t, rows, :].astype(F32) * y).astype(BF16)

    def second(t, rows, y):
        gated = x2_ref[t, rows, :].astype(F32) * y * g_ref[t, rows, :].astype(F32)
        o_ref[t, rows, :] = gated.astype(o_ref.dtype)

    for n, (z_in_ref, finish) in enumerate(((v_ref, first), (z_ref, second))):
        nyq_in = [forward(t, z_in_ref) for t in range(nt)]
        for t in range(nt):
            pair_products(t, n)
        for t in range(nt):
            inverse(t, z_in_ref, n, nyq_in[t], finish)


def _longconv(u3, g, cs, hr, hn, hq, skip):
    b, L, e3 = u3.shape
    e = e3 // 3
    blk = cs.shape[0]
    n_lags = hr.shape[1]
    te = 256
    ne = e // te
    nt = 2 if b % 2 == 0 else 1
    seq = lambda off: pl.BlockSpec((nt, L, te), lambda ei, bi: (bi, 0, off * ne + ei))
    spec = lambda: pl.BlockSpec((HY_ORDER, n_lags, blk, te), lambda ei, bi: (0, 0, 0, ei),
                                pipeline_mode=pl.Buffered(1))
    return pl.pallas_call(
        functools.partial(_longconv_kernel, blk=blk),
        grid=(ne, b // nt),
        in_specs=[pl.BlockSpec((blk, 2 * blk), lambda ei, bi: (0, 0)),
                  seq(0), seq(1), seq(2),
                  pl.BlockSpec((nt, L, te), lambda ei, bi: (bi, 0, ei)),
                  spec(), spec(),
                  pl.BlockSpec((HY_ORDER, hq.shape[1], te), lambda ei, bi: (0, 0, ei)),
                  pl.BlockSpec((HY_ORDER, 1, te), lambda ei, bi: (0, 0, ei))],
        out_specs=pl.BlockSpec((nt, L, te), lambda ei, bi: (bi, 0, ei)),
        out_shape=jax.ShapeDtypeStruct((b, L, e), BF16),
        scratch_shapes=[pltpu.VMEM((nt, L, te), F32), pltpu.VMEM((nt, L, te), F32),
                        pltpu.VMEM((nt, L, te), BF16), pltpu.VMEM((nt, L, te), BF16),
                        pltpu.VMEM((nt, L, te), BF16)],
        compiler_params=_params("arbitrary", "arbitrary"),
        name="hyena_longconv",
    )(cs, u3, u3, u3, g, hr, hn, hq, skip.reshape(HY_ORDER, 1, e))


def _cfmid_kernel(cur_ref, prev_ref, next_ref, g_ref, w_ref, b_ref, lg_ref, lb_ref, o_ref,
                  pad_ref, sh_ref, cv_ref, *, n_tiles):
    i = pl.program_id(1)
    tl, e = cur_ref.shape
    pad_ref[0:CF_HALO, :] = jnp.where(i > 0, prev_ref[...].astype(F32), 0.0)
    pad_ref[CF_HALO:CF_HALO + tl, :] = cur_ref[...].astype(F32)
    pad_ref[CF_HALO + tl:2 * CF_HALO + tl, :] = jnp.where(i < n_tiles - 1, next_ref[...].astype(F32), 0.0)
    rc = 64
    lc = sh_ref.shape[-1]
    sh_rows = sh_ref.shape[1]
    first = CF_HALO - (CF_KERNEL - 1) // 2

    def lane_body(li, carry):
        c0 = pl.multiple_of(li * lc, lc)
        lanes = pl.ds(c0, lc)
        for s in range(1, SUBLANES):
            sh_ref[s - 1] = pad_ref[pl.ds(s, sh_rows), lanes]
        bias = b_ref[:, lanes]
        for r in range(tl // rc):
            acc = jnp.broadcast_to(bias, (rc, lc))
            for k in range(CF_KERNEL):
                phase = (first + k) % SUBLANES
                base = r * rc + first + k - phase
                if phase == 0:
                    tap = pad_ref[pl.ds(base, rc), lanes]
                else:
                    tap = sh_ref[phase - 1, pl.ds(base, rc), :]
                acc = acc + tap * w_ref[pl.ds(k, 1), lanes]
            cv_ref[pl.ds(r * rc, rc), lanes] = acc
        return carry

    lax.fori_loop(0, e // lc, lane_body, 0)
    ng = min(tl, 128)

    def norm_body(ri, carry):
        rows = pl.ds(pl.multiple_of(ri * ng, ng), ng)
        cv = cv_ref[rows, :]
        mu = jnp.mean(cv, axis=-1, keepdims=True)
        xc = cv - mu
        var = jnp.mean(xc * xc, axis=-1, keepdims=True)
        y = xc * lax.rsqrt(var + EPS) * lg_ref[...] + lb_ref[...]
        o_ref[rows, :] = (_silu(y) * g_ref[rows, :].astype(F32)).astype(o_ref.dtype)
        return carry

    lax.fori_loop(0, tl // ng, norm_body, 0)


def _cfmid(u, g, dw_w, dw_b, ln_g, ln_b):
    b, L, e = u.shape
    tl = min(L, 256)
    n_tiles = L // tl
    hb = tl // CF_HALO
    n_halo = L // CF_HALO
    wpad = jnp.pad(dw_w, ((0, 32 - CF_KERNEL), (0, 0)))
    vec = lambda: pl.BlockSpec((1, e), lambda bi, i: (0, 0))
    return pl.pallas_call(
        functools.partial(_cfmid_kernel, n_tiles=n_tiles),
        grid=(b, n_tiles),
        in_specs=[pl.BlockSpec((None, tl, e), lambda bi, i: (bi, i, 0)),
                  pl.BlockSpec((None, CF_HALO, e), lambda bi, i: (bi, jnp.maximum(i * hb - 1, 0), 0)),
                  pl.BlockSpec((None, CF_HALO, e), lambda bi, i: (bi, jnp.minimum((i + 1) * hb, n_halo - 1), 0)),
                  pl.BlockSpec((None, tl, e), lambda bi, i: (bi, i, 0)),
                  pl.BlockSpec((32, e), lambda bi, i: (0, 0)),
                  vec(), vec(), vec()],
        out_specs=pl.BlockSpec((None, tl, e), lambda bi, i: (bi, i, 0)),
        out_shape=jax.ShapeDtypeStruct((b, L, e), BF16),
        scratch_shapes=[pltpu.VMEM((tl + 2 * CF_HALO, e), F32),
                        pltpu.VMEM((SUBLANES - 1, tl + 2 * CF_HALO - SUBLANES, 512), F32),
                        pltpu.VMEM((tl, e), F32)],
        compiler_params=_params("arbitrary", "arbitrary"),
        name="conformer_mid",
    )(u, u, u, g, wpad, dw_b.reshape(1, e), ln_g.reshape(1, e), ln_b.reshape(1, e))


def _retention_kernel(lg_ref, q_ref, k_ref, v_ref, g_ref, kc_ref, vc_ref, gg_ref, gb_ref, o_ref,
                      state_ref, oacc_ref, *, chunk, hp):
    h0 = pl.program_id(1) * hp
    L = q_ref.shape[0]
    dk = q_ref.shape[-1] // hp
    dv = v_ref.shape[-1] // hp
    n_lat = L // chunk
    n_ctx = kc_ref.shape[0] // chunk
    rowk = lax.broadcasted_iota(jnp.int32, (chunk, dk), 0).astype(F32)
    ri = lax.broadcasted_iota(jnp.int32, (chunk, chunk), 0)
    ci = lax.broadcasted_iota(jnp.int32, (chunk, chunk), 1)
    kcols = lambda hh: slice(hh * dk, (hh + 1) * dk)
    vcols = lambda hh: slice(hh * dv, (hh + 1) * dv)

    def run(direction):
        fwd = direction == 0
        q_dec, k_dec, dmask, c_dec = [], [], [], []
        for hh in range(hp):
            lg = lg_ref[direction, h0 + hh]
            if fwd:
                q_dec.append(jnp.exp(lg * (rowk + 1.0)))
                k_dec.append(jnp.exp(lg * (chunk - 1.0 - rowk)))
                rel = (ri - ci).astype(F32)
            else:
                q_dec.append(jnp.exp(lg * (chunk - rowk)))
                k_dec.append(jnp.exp(lg * rowk))
                rel = (ci - ri).astype(F32)
            dmask.append(jnp.where(rel >= 0, jnp.exp(lg * jnp.maximum(rel, 0.0)), 0.0))
            c_dec.append(jnp.exp(jnp.full((1, dv), lg * chunk, F32)))

        def state_update(hh, st, kc, vc):
            kd = (kc.astype(F32) * k_dec[hh]).T.astype(BF16)
            return st * c_dec[hh] + jnp.dot(kd, vc, preferred_element_type=F32)

        state_ref[...] = jnp.zeros_like(state_ref)

        def ctx_body(j, carry):
            c = j if fwd else n_ctx - 1 - j
            rows = pl.ds(pl.multiple_of(c * chunk, chunk), chunk)
            for hh in range(hp):
                state_ref[hh] = state_update(hh, state_ref[hh], kc_ref[rows, kcols(hh)], vc_ref[rows, vcols(hh)])
            return carry

        lax.fori_loop(0, n_ctx, ctx_body, 0)

        def lat_body(j, carry):
            c = j if fwd else n_lat - 1 - j
            rows = pl.ds(pl.multiple_of(c * chunk, chunk), chunk)
            for hh in range(hp):
                qc = q_ref[rows, kcols(hh)]
                kc = k_ref[rows, kcols(hh)]
                vc = v_ref[rows, vcols(hh)]
                st = state_ref[hh]
                scores = lax.dot_general(qc, kc, (((1,), (1,)), ((), ())), preferred_element_type=F32) * dmask[hh]
                out = (jnp.dot(scores.astype(BF16), vc, preferred_element_type=F32)
                       + jnp.dot((qc.astype(F32) * q_dec[hh]).astype(BF16), st.astype(BF16),
                                 preferred_element_type=F32))
                state_ref[hh] = state_update(hh, st, kc, vc)
                if fwd:
                    oacc_ref[rows, vcols(hh)] = out
                else:
                    o = oacc_ref[rows, vcols(hh)] + out
                    mu = jnp.mean(o, axis=-1, keepdims=True)
                    oc = o - mu
                    var = jnp.mean(oc * oc, axis=-1, keepdims=True)
                    y = oc * lax.rsqrt(var + EPS) * gg_ref[:, vcols(hh)] + gb_ref[:, vcols(hh)]
                    o_ref[rows, vcols(hh)] = (y * g_ref[rows, vcols(hh)].astype(F32)).astype(o_ref.dtype)
            return carry

        lax.fori_loop(0, n_lat, lat_body, 0)

    run(0)
    run(1)


def _retention(q, k, v, g, kc, vc, log_g, gn_g, gn_b):
    b, L, dqk = q.shape
    e = v.shape[-1]
    dk = dqk // RT_HEADS
    dv = e // RT_HEADS
    lc = kc.shape[1]
    chunk = 2 * RT_CHUNK if (L % (2 * RT_CHUNK) == 0 and lc % (2 * RT_CHUNK) == 0) else RT_CHUNK
    hp = 2
    head = lambda rows, width: pl.BlockSpec((None, rows, hp * width), lambda bi, hi: (bi, 0, hi))
    return pl.pallas_call(
        functools.partial(_retention_kernel, chunk=chunk, hp=hp),
        grid=(b, RT_HEADS // hp),
        in_specs=[pl.BlockSpec(memory_space=pltpu.SMEM),
                  head(L, dk), head(L, dk), head(L, dv), head(L, dv), head(lc, dk), head(lc, dv),
                  pl.BlockSpec((1, hp * dv), lambda bi, hi: (0, hi)),
                  pl.BlockSpec((1, hp * dv), lambda bi, hi: (0, hi))],
        out_specs=head(L, dv),
        out_shape=jax.ShapeDtypeStruct((b, L, e), BF16),
        scratch_shapes=[pltpu.VMEM((hp, dk, dv), F32), pltpu.VMEM((L, hp * dv), F32)],
        compiler_params=_params("arbitrary", "arbitrary"),
        name="retention",
    )(log_g, q, k, v, g, kc, vc, gn_g.reshape(1, e), gn_b.reshape(1, e))


def _rope_tables(L, dk):
    quarter = dk // 4
    inv = ROPE_BASE ** (-jnp.arange(quarter, dtype=F32) / quarter)
    t = jnp.arange(L)
    ar = (t // GRID_W).astype(F32)[:, None] * inv
    ac = (t % GRID_W).astype(F32)[:, None] * inv
    cos = jnp.concatenate([jnp.cos(ar), jnp.cos(ar), jnp.cos(ac), jnp.cos(ac)], axis=-1)
    sin = jnp.concatenate([-jnp.sin(ar), jnp.sin(ar), -jnp.sin(ac), jnp.sin(ac)], axis=-1)
    return cos, sin


def _hyena_branch(h, w_in, conv_w, conv_b, spec, bases, skip):
    b, L, d = h.shape
    e = w_in.shape[1] // 4
    h2 = h.reshape(b * L, d)
    u3 = _proj(h2, w_in, 0, 3 * e, "conv3", seq_len=L, extra=(conv_w, conv_b)).reshape(b, L, 3 * e)
    g = _proj(h2, w_in, 3 * e, e, "silu", seq_len=L).reshape(b, L, e)
    return _longconv(u3, g, bases, spec[0], spec[1], spec[2], skip)


def _conformer_branch(h, w_in, dw_w, dw_b, ln_g, ln_b):
    b, L, d = h.shape
    e = w_in.shape[1] // 3
    h2 = h.reshape(b * L, d)
    u = _proj(h2, w_in, 0, e, "glu", seq_len=L, extra=(e,)).reshape(b, L, e)
    g = _proj(h2, w_in, 2 * e, e, "silu", seq_len=L).reshape(b, L, e)
    return _cfmid(u, g, dw_w, dw_b, ln_g, ln_b)


def _retention_branch(h_lat, h_ctx, w_in, decay_logit, gn_g, gn_b):
    b, L, d = h_lat.shape
    lc = h_ctx.shape[1]
    dqk = d
    e = (w_in.shape[1] - 2 * dqk) // 2
    dk = dqk // RT_HEADS
    k_scale = dk ** -0.5
    hl = h_lat.reshape(b * L, d)
    hc = h_ctx.reshape(b * lc, d)
    cos, sin = _rope_tables(L, dk)
    q = _proj(hl, w_in, 0, dqk, "rope", seq_len=L, extra=(cos, sin)).reshape(b, L, dqk)
    k = _proj(hl, w_in, dqk, dqk, "rope", seq_len=L, extra=(cos, sin), scale=k_scale).reshape(b, L, dqk)
    v = _proj(hl, w_in, 2 * dqk, e, "plain", seq_len=L).reshape(b, L, e)
    g = _proj(hl, w_in, 2 * dqk + e, e, "silu", seq_len=L).reshape(b, L, e)
    kc = _proj(hc, w_in, dqk, dqk, "scale", seq_len=lc, scale=k_scale).reshape(b, lc, dqk)
    vc = _proj(hc, w_in, 2 * dqk, e, "plain", seq_len=lc).reshape(b, lc, e)
    log_g = jax.nn.log_sigmoid(decay_logit.astype(F32))
    return _retention(q, k, v, g, kc, vc, log_g, gn_g, gn_b)


def kernel(x, c, ctx, c_ctx, ada_w, ada_b, norm_g, final_norm_g,
           hy_w_in, hy_conv_w, hy_conv_b, hy_f_w1, hy_f_b1, hy_f_fr1, hy_f_w2, hy_f_b2,
           hy_f_fr2, hy_f_w3, hy_skip, hy_w_out,
           cf_w_in, cf_dw_w, cf_dw_b, cf_ln_g, cf_ln_b, cf_w_out,
           rt_w_in, rt_decay_logit, rt_gn_g, rt_gn_b, rt_w_out):
    depth = ada_w.shape[0]
    b, L, d = x.shape
    lc = ctx.shape[1]
    mod = _modulation(c, c_ctx, ada_w, ada_b)
    lat_row = lambda bi: bi
    ctx_row = lambda bi: b
    bases = {}
    h_lat = _normmod(x, norm_g[0], mod, 0, lat_row)
    h_ctx = _normmod(ctx, norm_g[0], mod, 0, ctx_row)
    for i in range(depth):
        kind, j = i % N_MIXERS, i // N_MIXERS
        last = i == depth - 1
        need_ctx_out = (not last) and not (i == depth - 2 and (depth - 1) % N_MIXERS != 2)
        need_ctx_in = need_ctx_out or kind == 2
        if need_ctx_in and h_ctx is None:
            raise NotImplementedError("this layer reads a context stream that no earlier layer produced")
        u_ctx = None
        if kind == 0:
            e = hy_w_in.shape[2] // 4
            w_in = hy_w_in[j].astype(BF16)
            w_out = hy_w_out[j].astype(BF16)
            filt = (hy_f_w1[j], hy_f_b1[j], hy_f_fr1[j], hy_f_w2[j], hy_f_b2[j], hy_f_fr2[j], hy_f_w3[j])
            for seq in (L, lc):
                if seq not in bases:
                    bases[seq] = _dft_bases(_conv_block(seq))
            spec = _hyena_spectrum(L, e, bases[L], *filt)
            u_lat = _hyena_branch(h_lat, w_in, hy_conv_w[j], hy_conv_b[j], spec, bases[L], hy_skip[j])
            if need_ctx_out:
                spec_c = _hyena_spectrum(lc, e, bases[lc], *filt)
                u_ctx = _hyena_branch(h_ctx, w_in, hy_conv_w[j], hy_conv_b[j], spec_c, bases[lc], hy_skip[j])
        elif kind == 1:
            w_in = cf_w_in[j].astype(BF16)
            w_out = cf_w_out[j].astype(BF16)
            p = (cf_dw_w[j], cf_dw_b[j], cf_ln_g[j], cf_ln_b[j])
            u_lat = _conformer_branch(h_lat, w_in, *p)
            if need_ctx_out:
                u_ctx = _conformer_branch(h_ctx, w_in, *p)
        else:
            w_in = rt_w_in[j].astype(BF16)
            w_out = rt_w_out[j].astype(BF16)
            if need_ctx_out:
                raise NotImplementedError("context outputs of a retention layer are not consumed by this stack")
            u_lat = _retention_branch(h_lat, h_ctx, w_in, rt_decay_logit[j], rt_gn_g[j], rt_gn_b[j])
        if last:
            return _outproj(u_lat, w_out, x, mod, i, lat_row, final_norm_g, None)
        x, h_lat = _outproj(u_lat, w_out, x, mod, i, lat_row, norm_g[i + 1], i + 1)
        if need_ctx_out:
            ctx, h_ctx = _outproj(u_ctx, w_out, ctx, mod, i, ctx_row, norm_g[i + 1], i + 1)
        else:
            h_ctx = None
```
